```python
import math
import jax, jax.numpy as jnp
from jax import lax
import numpy as np

D_MODEL = 1024
BATCH = 16
SEQ = 2048
DEPTH = 4

CHUNK = 64
Q_BLOCK = 128
SSD_HEADS = 8
SSD_HEAD_DIM = 64
SSD_WIDTH = SSD_HEADS * SSD_HEAD_DIM
SSD_GROUPS = 2
SSD_STATE = 128
SSD_CONV = 4
SSD_CHUNK = CHUNK
DIFF_HEADS = 4
DIFF_HEAD_DIM = 64
DIFF_V_DIM = 2 * DIFF_HEAD_DIM
DIFF_WIDTH = DIFF_HEADS * DIFF_V_DIM
ROT_DIM = DIFF_HEAD_DIM // 4
ROPE_THETA = 500000.0
MIX_WIDTH = SSD_WIDTH + DIFF_WIDTH
CONV_CH = SSD_WIDTH + 2 * SSD_GROUPS * SSD_STATE
QK_WIDTH = DIFF_HEADS * 2 * DIFF_HEAD_DIM
IN_SPLITS = (SSD_WIDTH, SSD_WIDTH + CONV_CH, SSD_WIDTH + CONV_CH + SSD_HEADS,
             SSD_WIDTH + CONV_CH + SSD_HEADS + QK_WIDTH,
             SSD_WIDTH + CONV_CH + SSD_HEADS + 2 * QK_WIDTH)
IN_COLS = SSD_WIDTH + CONV_CH + SSD_HEADS + 2 * QK_WIDTH + DIFF_WIDTH
D_FF = 2816
FFN_RES_WEIGHT = 0.5
DEEPNORM_ALPHA = (2 * DEPTH) ** 0.25
DEEPNORM_BETA = (8 * DEPTH) ** -0.25
N_SUB = 3
LN_EPS = 1e-5

kernel_name = "hybrid_ssd_diffattn_macaron_deepnorm_adaln"


def _standardize(x):
    xf = x.astype(jnp.float32)
    mu = jnp.mean(xf, axis=-1, keepdims=True)
    var = jnp.mean(jnp.square(xf - mu), axis=-1, keepdims=True)
    return ((xf - mu) * lax.rsqrt(var + LN_EPS)).astype(x.dtype)


def layer_norm(x, g, b):
    return _standardize(x) * g + b


def rms_norm(x, w):
    xf = x.astype(jnp.float32)
    y = xf * lax.rsqrt(jnp.mean(jnp.square(xf), axis=-1, keepdims=True) + LN_EPS)
    return y.astype(x.dtype) * w


def modulate(x, shift, scale):
    return _standardize(x) * (1 + scale) + shift


def swiglu(u, w_in, w_out):
    g, up = jnp.split(u @ w_in, 2, axis=-1)
    return (jax.nn.silu(g) * up) @ w_out


def causal_depthwise_conv(x, w, b):
    out = lax.conv_general_dilated(
        x, w[:, None, :], window_strides=(1,), padding=[(SSD_CONV - 1, 0)],
        dimension_numbers=('NWC', 'WIO', 'NWC'), feature_group_count=x.shape[-1])
    return out + b


def ssd_scan(x, dt, A, Bm, Cm, d_skip):
    b, L = x.shape[:2]
    nc = L // SSD_CHUNK
    r = SSD_HEADS // SSD_GROUPS
    dt_x = dt.astype(x.dtype)
    xdt = (x * dt_x[..., None]).reshape(b, nc, SSD_CHUNK, SSD_GROUPS, r, SSD_HEAD_DIM)
    a = (dt * A).reshape(b, nc, SSD_CHUNK, SSD_GROUPS, r)
    a_cs = jnp.cumsum(a, axis=2)
    Bc = Bm.reshape(b, nc, SSD_CHUNK, SSD_GROUPS, SSD_STATE)
    Cc = Cm.reshape(b, nc, SSD_CHUNK, SSD_GROUPS, SSD_STATE)
    seg = a_cs[:, :, :, None] - a_cs[:, :, None, :]
    causal = jnp.tril(jnp.ones((SSD_CHUNK, SSD_CHUNK), dtype=bool))[:, :, None, None]
    decay = jnp.exp(jnp.where(causal, seg, -jnp.inf)).astype(x.dtype)
    y_diag = jnp.einsum('bclgn,bcsgn,bclsgr,bcsgrp->bclgrp', Cc, Bc, decay, xdt)
    decay_to_end = jnp.exp(a_cs[:, :, -1:] - a_cs).astype(x.dtype)
    states = jnp.einsum('bclgn,bclgr,bclgrp->bcgrpn', Bc, decay_to_end, xdt)
    chunk_decay = jnp.exp(a_cs[:, :, -1]).astype(x.dtype)

    def step(h, inp):
        st, dec = inp
        return h * dec[..., None, None] + st, h

    h0 = jnp.zeros((b, SSD_GROUPS, r, SSD_HEAD_DIM, SSD_STATE), states.dtype)
    _, prev = lax.scan(step, h0, (jnp.moveaxis(states, 1, 0), jnp.moveaxis(chunk_decay, 1, 0)))
    prev = jnp.moveaxis(prev, 0, 1)
    y_off = jnp.einsum('bclgn,bcgrpn,bclgr->bclgrp', Cc, prev, jnp.exp(a_cs).astype(x.dtype))
    y = (y_diag + y_off).reshape(b, L, SSD_HEADS, SSD_HEAD_DIM)
    return y + x * d_skip[:, None]


def partial_rope(t, pos):
    half = ROT_DIM // 2
    inv = ROPE_THETA ** (-jnp.arange(half, dtype=jnp.float32) * 2.0 / ROT_DIM)
    ang = pos.astype(jnp.float32)[..., None] * inv
    cos = jnp.cos(ang)[:, :, None, None, :].astype(t.dtype)
    sin = jnp.sin(ang)[:, :, None, None, :].astype(t.dtype)
    t1, t2, rest = t[..., :half], t[..., half:ROT_DIM], t[..., ROT_DIM:]
    return jnp.concatenate([t1 * cos - t2 * sin, t2 * cos + t1 * sin, rest], axis=-1)


def diff_attention(q, k, v, lam, lam_init, subln_w):
    b, L = q.shape[:2]
    nblk = L // Q_BLOCK
    scale = DIFF_HEAD_DIM ** -0.5
    kchunk = jnp.arange(L) // CHUNK
    qb = jnp.moveaxis(q.reshape(b, nblk, Q_BLOCK, DIFF_HEADS, 2, DIFF_HEAD_DIM), 1, 0)

    def block(args):
        qi, i = args
        s = jnp.einsum('bqhjd,bkhjd->bhjqk', qi, k).astype(jnp.float32) * scale
        qchunk = (i * Q_BLOCK + jnp.arange(Q_BLOCK)) // CHUNK
        mask = kchunk[None, :] <= qchunk[:, None]
        p = jax.nn.softmax(jnp.where(mask, s, -jnp.inf), axis=-1)
        attn = p[:, :, 0] - lam * p[:, :, 1]
        return jnp.einsum('bhqk,bkhe->bqhe', attn.astype(v.dtype), v)

    o = lax.map(block, (qb, jnp.arange(nblk)))
    o = jnp.moveaxis(o, 0, 1).reshape(b, L, DIFF_HEADS, DIFF_V_DIM)
    o = rms_norm(o, subln_w) * (1.0 - lam_init)
    return o.reshape(b, L, DIFF_WIDTH)


def hybrid_mixer(u, pos, w_in, conv_w, conv_b, dt_bias, a_log, d_skip, ssd_norm_w,
                 diff_lambda, subln_w, w_out, lam_init):
    b, L, _ = u.shape
    proj = u @ w_in
    z, xbc, dt_raw, q, k, v = jnp.split(proj, IN_SPLITS, axis=-1)
    xbc = jax.nn.silu(causal_depthwise_conv(xbc, conv_w, conv_b))
    xs, Bm, Cm = jnp.split(xbc, (SSD_WIDTH, SSD_WIDTH + SSD_GROUPS * SSD_STATE), axis=-1)
    dt = jax.nn.softplus((dt_raw + dt_bias).astype(jnp.float32))
    A = -jnp.exp(a_log.astype(jnp.float32))
    y = ssd_scan(xs.reshape(b, L, SSD_HEADS, SSD_HEAD_DIM), dt, A,
                 Bm.reshape(b, L, SSD_GROUPS, SSD_STATE), Cm.reshape(b, L, SSD_GROUPS, SSD_STATE), d_skip)
    y = y.reshape(b, L, SSD_WIDTH) * jax.nn.silu(z)
    y = rms_norm(y.reshape(b, L, SSD_GROUPS, SSD_WIDTH // SSD_GROUPS),
                 ssd_norm_w.reshape(SSD_GROUPS, SSD_WIDTH // SSD_GROUPS)).reshape(b, L, SSD_WIDTH)
    q = partial_rope(q.reshape(b, L, DIFF_HEADS, 2, DIFF_HEAD_DIM), pos)
    k = partial_rope(k.reshape(b, L, DIFF_HEADS, 2, DIFF_HEAD_DIM), pos)
    v = v.reshape(b, L, DIFF_HEADS, DIFF_V_DIM)
    lp = diff_lambda.astype(jnp.float32)
    lam = jnp.exp(jnp.sum(lp[0] * lp[1])) - jnp.exp(jnp.sum(lp[2] * lp[3])) + lam_init
    o = diff_attention(q, k, v, lam, lam_init, subln_w)
    return jnp.concatenate([y, o], axis=-1) @ w_out


def setup_inputs(seed: int = 0) -> dict:
    key = jax.random.key(seed)
    ks = jax.random.split(key, 20)
    f32 = jnp.float32
    x = jax.random.normal(ks[0], (BATCH, SEQ, D_MODEL), f32)
    c = jax.random.normal(ks[1], (BATCH, D_MODEL), f32)
    offset = jax.random.randint(ks[2], (BATCH, 1), 0, 64, dtype=jnp.int32) * CHUNK
    positions = (offset + jnp.arange(SEQ, dtype=jnp.int32)[None, :]).astype(jnp.int32)
    w_ada = jax.random.normal(ks[3], (DEPTH, D_MODEL, N_SUB * 3 * D_MODEL), f32) * (0.2 * D_MODEL ** -0.5)
    b_ada = jax.random.normal(ks[4], (DEPTH, N_SUB * 3 * D_MODEL), f32) * 0.01
    w_ffn_in = jax.random.normal(ks[5], (DEPTH, 2, D_MODEL, 2 * D_FF), f32) * D_MODEL ** -0.5
    w_ffn_out = jax.random.normal(ks[6], (DEPTH, 2, D_FF, D_MODEL), f32) * (D_FF ** -0.5 * DEEPNORM_BETA)
    w_in = jax.random.normal(ks[7], (DEPTH, D_MODEL, IN_COLS), f32) * D_MODEL ** -0.5
    conv_w = jax.random.normal(ks[8], (DEPTH, SSD_CONV, CONV_CH), f32) * SSD_CONV ** -0.5
    conv_b = jax.random.normal(ks[9], (DEPTH, CONV_CH), f32) * 0.01
    u_dt = jax.random.uniform(ks[10], (DEPTH, SSD_HEADS), f32)
    dt0 = jnp.exp(u_dt * (math.log(0.1) - math.log(0.001)) + math.log(0.001))
    dt0 = jnp.maximum(dt0, 1e-4)
    dt_bias = dt0 + jnp.log(-jnp.expm1(-dt0))
    a_log = jnp.log(jax.random.uniform(ks[11], (DEPTH, SSD_HEADS), f32, 1.0, 16.0))
    d_skip = 1.0 + 0.01 * jax.random.normal(ks[12], (DEPTH, SSD_HEADS), f32)
    ssd_norm_w = 1.0 + 0.01 * jax.random.normal(ks[13], (DEPTH, SSD_WIDTH), f32)
    diff_lambda = 0.1 * jax.random.normal(ks[14], (DEPTH, 4, DIFF_HEAD_DIM), f32)
    subln_w = 1.0 + 0.01 * jax.random.normal(ks[15], (DEPTH, DIFF_V_DIM), f32)
    w_out = jax.random.normal(ks[16], (DEPTH, MIX_WIDTH, D_MODEL), f32) * (MIX_WIDTH ** -0.5 * DEEPNORM_BETA)
    ln_g = 1.0 + 0.01 * jax.random.normal(ks[17], (DEPTH, N_SUB, D_MODEL), f32)
    ln_b = 0.01 * jax.random.normal(ks[18], (DEPTH, N_SUB, D_MODEL), f32)
    return {"x": x, "c": c, "positions": positions, "w_ada": w_ada, "b_ada": b_ada,
            "w_ffn_in": w_ffn_in, "w_ffn_out": w_ffn_out, "w_in": w_in, "conv_w": conv_w,
            "conv_b": conv_b, "dt_bias": dt_bias, "a_log": a_log, "d_skip": d_skip,
            "ssd_norm_w": ssd_norm_w, "diff_lambda": diff_lambda, "subln_w": subln_w,
            "w_out": w_out, "ln_g": ln_g, "ln_b": ln_b}


def reference(x, c, positions, w_ada, b_ada, w_ffn_in, w_ffn_out, w_in, conv_w, conv_b,
              dt_bias, a_log, d_skip, ssd_norm_w, diff_lambda, subln_w, w_out, ln_g, ln_b):
    b = x.shape[0]
    for layer in range(DEPTH):
        mod = (jax.nn.silu(c) @ w_ada[layer] + b_ada[layer]).reshape(b, N_SUB, 3, 1, D_MODEL)
        lam_init = 0.8 - 0.6 * math.exp(-0.3 * layer)
        u = modulate(x, mod[:, 0, 0], mod[:, 0, 1])
        f = swiglu(u, w_ffn_in[layer, 0], w_ffn_out[layer, 0])
        x = layer_norm(DEEPNORM_ALPHA * x + FFN_RES_WEIGHT * (1 + mod[:, 0, 2]) * f,
                       ln_g[layer, 0], ln_b[layer, 0])
        u = modulate(x, mod[:, 1, 0], mod[:, 1, 1])
        m = hybrid_mixer(u, positions, w_in[layer], conv_w[layer], conv_b[layer], dt_bias[layer],
                         a_log[layer], d_skip[layer], ssd_norm_w[layer], diff_lambda[layer],
                         subln_w[layer], w_out[layer], lam_init)
        x = layer_norm(DEEPNORM_ALPHA * x + (1 + mod[:, 1, 2]) * m, ln_g[layer, 1], ln_b[layer, 1])
        u = modulate(x, mod[:, 2, 0], mod[:, 2, 1])
        f = swiglu(u, w_ffn_in[layer, 1], w_ffn_out[layer, 1])
        x = layer_norm(DEEPNORM_ALPHA * x + FFN_RES_WEIGHT * (1 + mod[:, 2, 2]) * f,
                       ln_g[layer, 2], ln_b[layer, 2])
    return x
```

```python
import functools

import numpy as np
import jax
import jax.numpy as jnp
from jax import lax
from jax.experimental import pallas as pl
from jax.experimental.pallas import tpu as pltpu

F32 = jnp.float32
BF16 = jnp.bfloat16

D_MODEL = 1024
DEPTH = 4
CHUNK = 64
SSD_HEADS = 8
SSD_HEAD_DIM = 64
SSD_WIDTH = 512
SSD_GROUPS = 2
SSD_STATE = 128
SSD_CONV = 4
DIFF_HEADS = 4
DIFF_HEAD_DIM = 64
DIFF_V_DIM = 128
DIFF_WIDTH = 512
ROT_DIM = 16
ROPE_THETA = 500000.0
CONV_CH = 1024
QK_WIDTH = 512
D_FF = 2816
N_SUB = 3
LN_EPS = 1e-5
DEEPNORM_ALPHA = (2 * DEPTH) ** 0.25
FFN_RES_WEIGHT = 0.5

LANES = 128
SUBLANES = 8
VMEM_LIMIT = 56 * 1024 * 1024

TOKEN_TILE = 512
FFN_CHUNKS = 11
ATTN_Q_TILE = 256
SSD_TILE = 256
SSD_Q = 128
DT_PAD = LANES
PROJ_COLS = SSD_WIDTH + CONV_CH + 2 * QK_WIDTH + DIFF_WIDTH + DT_PAD


def _params(*sem):
    return pltpu.CompilerParams(dimension_semantics=sem, vmem_limit_bytes=VMEM_LIMIT)


def _const_spec(shape):
    nd = len(shape)
    return pl.BlockSpec(shape, lambda *_: (0,) * nd, pipeline_mode=pl.Buffered(1))


def _standardize(x):
    mu = jnp.mean(x, axis=-1, keepdims=True)
    xc = x - mu
    var = jnp.mean(xc * xc, axis=-1, keepdims=True)
    return xc * lax.rsqrt(var + LN_EPS)


def _silu(x):
    return x * jax.nn.sigmoid(x)


def _dot(a, b):
    return jnp.dot(a, b, preferred_element_type=F32)


def _dot_nt(a, b):
    return lax.dot_general(a, b, (((1,), (1,)), ((), ())), preferred_element_type=F32)


def _dot_tn(a, b):
    return lax.dot_general(a, b, (((0,), (0,)), ((), ())), preferred_element_type=F32)


def _dot_exact(a, b):
    return jnp.dot(a, b, preferred_element_type=F32, precision=lax.Precision.HIGHEST)


def _ada_kernel(c_ref, w_ref, b_ref, o_ref):
    sc = _silu(c_ref[...]).astype(BF16)
    o_ref[...] = _dot(sc, w_ref[...].astype(BF16)) + b_ref[...]


def _ada_mod(c, w_ada, b_ada):
    B = c.shape[0]
    ncol = w_ada.shape[-1]
    tn = 1024
    return pl.pallas_call(
        _ada_kernel,
        grid=(DEPTH, ncol // tn),
        in_specs=[
            pl.BlockSpec((B, D_MODEL), lambda l, j: (0, 0)),
            pl.BlockSpec((None, D_MODEL, tn), lambda l, j: (l, 0, j)),
            pl.BlockSpec((None, 1, tn), lambda l, j: (l, 0, j)),
        ],
        out_specs=pl.BlockSpec((None, B, tn), lambda l, j: (l, 0, j)),
        out_shape=jax.ShapeDtypeStruct((DEPTH, B, ncol), F32),
        compiler_params=_params("arbitrary", "arbitrary"),
        name="ada_mod",
    )(c, w_ada, b_ada.reshape(DEPTH, 1, ncol))


def _rope_kernel(pos_ref, inv_ref, cos_ref, sin_ref):
    ang = pos_ref[...].astype(F32) * inv_ref[...]
    d = lax.broadcasted_iota(jnp.int32, ang.shape, 1) & (DIFF_HEAD_DIM - 1)
    c = jnp.cos(ang)
    s = jnp.sin(ang)
    cos_ref[...] = jnp.where(d < ROT_DIM, c, 1.0)
    sin_ref[...] = jnp.where(d < ROT_DIM // 2, -s, jnp.where(d < ROT_DIM, s, 0.0))


def _rope_tables(positions):
    B, L = positions.shape
    half = ROT_DIM // 2
    inv = np.float32(ROPE_THETA) ** (-np.arange(half, dtype=np.float32) * np.float32(2.0) / np.float32(ROT_DIM))
    inv_lane = jnp.asarray(np.tile(inv.astype(np.float32), LANES // half)[None, :])
    tm = min(L, 1024)
    spec = pl.BlockSpec((None, tm, LANES), lambda b, i: (b, i, 0))
    return pl.pallas_call(
        _rope_kernel,
        grid=(B, L // tm),
        in_specs=[pl.BlockSpec((None, tm, 1), lambda b, i: (b, i, 0)),
                  pl.BlockSpec((1, LANES), lambda b, i: (0, 0))],
        out_specs=[spec, spec],
        out_shape=[jax.ShapeDtypeStruct((B, L, LANES), F32)] * 2,
        compiler_params=_params("arbitrary", "arbitrary"),
        name="rope_tables",
    )(positions.reshape(B, L, 1), inv_lane)


def _ffn_kernel(x_ref, mod_ref, win_ref, wout_ref, lng_ref, lnb_ref, o_ref, *, sub):
    x = x_ref[...]
    shift = mod_ref[3 * sub:3 * sub + 1, :]
    scale = mod_ref[3 * sub + 1:3 * sub + 2, :]
    gate = mod_ref[3 * sub + 2:3 * sub + 3, :]
    u = (_standardize(x) * (1.0 + scale) + shift).astype(BF16)
    fc = D_FF // FFN_CHUNKS
    f = None
    for k in range(FFN_CHUNKS):
        g = _dot(u, win_ref[:, k * fc:(k + 1) * fc])
        up = _dot(u, win_ref[:, D_FF + k * fc:D_FF + (k + 1) * fc])
        h = (_silu(g) * up).astype(BF16)
        fk = _dot(h, wout_ref[k * fc:(k + 1) * fc, :])
        f = fk if f is None else f + fk
    y = DEEPNORM_ALPHA * x + (FFN_RES_WEIGHT * (1.0 + gate)) * f
    o_ref[...] = _standardize(y) * lng_ref[...] + lnb_ref[...]


def _ffn(x, mod, w_in, w_out, ln_g, ln_b, sub):
    B, L, _ = x.shape
    tm = min(L, TOKEN_TILE)
    xspec = pl.BlockSpec((None, tm, D_MODEL), lambda b, i: (b, i, 0))
    return pl.pallas_call(
        functools.partial(_ffn_kernel, sub=sub),
        grid=(B, L // tm),
        in_specs=[
            xspec,
            pl.BlockSpec((None, 3 * N_SUB, D_MODEL), lambda b, i: (b, 0, 0)),
            _const_spec((D_MODEL, 2 * D_FF)),
            _const_spec((D_FF, D_MODEL)),
            _const_spec((1, D_MODEL)),
            _const_spec((1, D_MODEL)),
        ],
        out_specs=xspec,
        out_shape=jax.ShapeDtypeStruct(x.shape, F32),
        compiler_params=_params("arbitrary", "arbitrary"),
        name=f"ffn{sub}",
    )(x, mod, w_in, w_out, ln_g, ln_b)


_Z0 = 0
_XBC0 = SSD_WIDTH
_Q0 = _XBC0 + CONV_CH
_K0 = _Q0 + QK_WIDTH
_V0 = _K0 + QK_WIDTH
_DT0 = _V0 + DIFF_WIDTH


def _inproj_kernel(x_ref, mod_ref, w_ref, cos_ref, sin_ref,
                   z_ref, xbc_ref, q_ref, k_ref, v_ref, dt_ref):
    x = x_ref[...]
    shift = mod_ref[3:4, :]
    scale = mod_ref[4:5, :]
    u = (_standardize(x) * (1.0 + scale) + shift).astype(BF16)
    z_ref[...] = _dot(u, w_ref[:, _Z0:_XBC0])
    xbc_ref[...] = _dot(u, w_ref[:, _XBC0:_Q0])
    v_ref[...] = _dot(u, w_ref[:, _V0:_DT0]).astype(BF16)
    dt_ref[...] = _dot(u, w_ref[:, _DT0:_DT0 + DT_PAD])
    cos = cos_ref[...]
    sin = sin_ref[...]
    d = lax.broadcasted_iota(jnp.int32, cos.shape, 1) & (DIFF_HEAD_DIM - 1)
    first_half = d < ROT_DIM // 2
    half = ROT_DIM // 2
    qk_scale = DIFF_HEAD_DIM ** -0.5
    for h in range(DIFF_HEADS):
        lo = h * LANES
        for col0, ref, mult in ((_Q0, q_ref, qk_scale), (_K0, k_ref, 1.0)):
            t = _dot(u, w_ref[:, col0 + lo:col0 + lo + LANES])
            partner = jnp.where(first_half, pltpu.roll(t, LANES - half, axis=1), pltpu.roll(t, half, axis=1))
            r = t * cos + partner * sin
            ref[:, lo:lo + LANES] = (r * mult).astype(BF16)


def _inproj(x, mod, w, cos, sin):
    B, L, _ = x.shape
    tm = min(L, TOKEN_TILE)

    def tok(width):
        return pl.BlockSpec((None, tm, width), lambda b, i: (b, i, 0))

    return pl.pallas_call(
        _inproj_kernel,
        grid=(B, L // tm),
        in_specs=[
            tok(D_MODEL),
            pl.BlockSpec((None, 3 * N_SUB, D_MODEL), lambda b, i: (b, 0, 0)),
            _const_spec((D_MODEL, PROJ_COLS)),
            tok(LANES), tok(LANES),
        ],
        out_specs=[tok(SSD_WIDTH), tok(CONV_CH), tok(QK_WIDTH), tok(QK_WIDTH), tok(DIFF_WIDTH), tok(DT_PAD)],
        out_shape=[
            jax.ShapeDtypeStruct((B, L, SSD_WIDTH), F32),
            jax.ShapeDtypeStruct((B, L, CONV_CH), F32),
            jax.ShapeDtypeStruct((B, L, QK_WIDTH), BF16),
            jax.ShapeDtypeStruct((B, L, QK_WIDTH), BF16),
            jax.ShapeDtypeStruct((B, L, DIFF_WIDTH), BF16),
            jax.ShapeDtypeStruct((B, L, DT_PAD), F32),
        ],
        compiler_params=_params("arbitrary", "arbitrary"),
        name="mixer_inproj",
    )(x, mod, w, cos, sin)


def _ssd_kernel(xbc_ref, dt_ref, z_ref, convw_ref, convb_ref, dtb_ref, alog_ref, dskip_ref, normw_ref,
                y_ref, xpad_ref, state_ref, *, tl):
    halo = SUBLANES
    hg = SSD_HEADS // SSD_GROUPS
    gw = hg * SSD_HEAD_DIM

    @pl.when(pl.program_id(1) == 0)
    def _():
        xpad_ref[0:halo, :] = jnp.zeros((halo, CONV_CH), F32)
        state_ref[...] = jnp.zeros(state_ref.shape, F32)

    xpad_ref[halo:halo + tl, :] = xbc_ref[...]
    conv = convb_ref[...]
    for k in range(SSD_CONV):
        off = halo - (SSD_CONV - 1) + k
        conv = conv + convw_ref[k:k + 1, :] * xpad_ref[off:off + tl, :]
    xpad_ref[0:halo, :] = xpad_ref[tl:tl + halo, :]
    act = _silu(conv)
    xs = act[:, :SSD_WIDTH]
    bm = act[:, SSD_WIDTH:SSD_WIDTH + SSD_GROUPS * SSD_STATE].astype(BF16)
    cm = act[:, SSD_WIDTH + SSD_GROUPS * SSD_STATE:].astype(BF16)

    lane = lax.broadcasted_iota(jnp.int32, (1, LANES), 1)
    v = dt_ref[...] + dtb_ref[...]
    dt = jnp.maximum(v, 0.0) + jnp.log1p(jnp.exp(-jnp.abs(v)))
    a = jnp.where(lane < SSD_HEADS, dt * (-jnp.exp(alog_ref[...])), 0.0)

    row_h = lax.broadcasted_iota(jnp.int32, (LANES, SSD_WIDTH), 0)
    col_h = lax.broadcasted_iota(jnp.int32, (LANES, SSD_WIDTH), 1) >> 6
    e_head = (row_h == col_h).astype(F32)
    row_t = lax.broadcasted_iota(jnp.int32, (LANES, SSD_HEADS * LANES), 0)
    col_t = lax.broadcasted_iota(jnp.int32, (LANES, SSD_HEADS * LANES), 1) >> 7
    e_tile = (row_t == col_t).astype(F32)
    li = lax.broadcasted_iota(jnp.int32, (SSD_Q, SSD_Q), 0)
    si = lax.broadcasted_iota(jnp.int32, (SSD_Q, SSD_Q), 1)
    causal = li >= si
    tri = causal.astype(F32)

    xdt = xs * _dot_exact(dt, e_head)
    lo_half = lax.broadcasted_iota(jnp.int32, (1, LANES), 1) < SSD_HEAD_DIM

    y_blocks = []
    for c in range(tl // SSD_Q):
        r0 = c * SSD_Q
        acs = _dot_exact(tri, a[r0:r0 + SSD_Q, :])
        acs_t = acs.T
        acs_head = _dot_exact(acs, e_head)
        acs_tile = _dot_exact(acs, e_tile)
        last_head = acs_head[SSD_Q - 1:SSD_Q, :]
        grow = jnp.exp(acs_head)
        to_end = jnp.exp(last_head - acs_head)
        xdt_c = xdt[r0:r0 + SSD_Q, :]
        y_groups = []
        for g in range(SSD_GROUPS):
            bg = bm[r0:r0 + SSD_Q, g * SSD_STATE:(g + 1) * SSD_STATE]
            cg = cm[r0:r0 + SSD_Q, g * SSD_STATE:(g + 1) * SSD_STATE]
            cb = _dot_nt(cg, bg)
            st = state_ref[g]
            y_off = _dot_nt(cg, st.astype(BF16))
            y_pairs = []
            for p in range(hg // 2):
                ms, rhs = [], []
                xp = xdt_c[:, g * gw + p * LANES:g * gw + (p + 1) * LANES].astype(BF16)
                for j in range(2):
                    h = g * hg + 2 * p + j
                    seg = acs_tile[:, h * LANES:(h + 1) * LANES] - acs_t[h:h + 1, :]
                    decay = jnp.exp(jnp.where(causal, seg, -jnp.inf))
                    ms.append((cb * decay).astype(BF16))
                    keep = lo_half if j == 0 else jnp.logical_not(lo_half)
                    rhs.append(jnp.where(keep, xp, jnp.zeros_like(xp)))
                y_pairs.append(_dot(jnp.concatenate(ms, axis=1), jnp.concatenate(rhs, axis=0)))
            y_diag = jnp.concatenate(y_pairs, axis=1)
            y_groups.append(y_diag + y_off * grow[:, g * gw:(g + 1) * gw])
            wgt = (xdt_c[:, g * gw:(g + 1) * gw] * to_end[:, g * gw:(g + 1) * gw]).astype(BF16)
            d_state = _dot_tn(wgt, bg)
            scale_rows = []
            for j in range(hg):
                h = g * hg + j
                scale_rows.append(jnp.broadcast_to(jnp.exp(acs_t[h:h + 1, SSD_Q - 1:SSD_Q]), (SSD_HEAD_DIM, SSD_STATE)))
            state_ref[g] = st * jnp.concatenate(scale_rows, axis=0) + d_state
        y_blocks.append(jnp.concatenate(y_groups, axis=1))
    y = jnp.concatenate(y_blocks, axis=0) if len(y_blocks) > 1 else y_blocks[0]
    y = y + xs * dskip_ref[...]
    y = y * _silu(z_ref[...])
    outs = []
    for g in range(SSD_GROUPS):
        yg = y[:, g * gw:(g + 1) * gw]
        ms = jnp.mean(yg * yg, axis=-1, keepdims=True)
        outs.append(yg * lax.rsqrt(ms + LN_EPS) * normw_ref[:, g * gw:(g + 1) * gw])
    y_ref[...] = jnp.concatenate(outs, axis=1).astype(BF16)


def _ssd(xbc, dt_raw, z, conv_w, conv_b, dt_bias, a_log, d_skip, norm_w):
    B, L, _ = xbc.shape
    tl = min(L, SSD_TILE)

    def tok(width):
        return pl.BlockSpec((None, tl, width), lambda b, i: (b, i, 0))

    def pad_heads(p):
        return jnp.zeros((1, DT_PAD), F32).at[0, :SSD_HEADS].set(p)

    return pl.pallas_call(
        functools.partial(_ssd_kernel, tl=tl),
        grid=(B, L // tl),
        in_specs=[
            tok(CONV_CH), tok(DT_PAD), tok(SSD_WIDTH),
            _const_spec((SSD_CONV, CONV_CH)), _const_spec((1, CONV_CH)),
            _const_spec((1, DT_PAD)), _const_spec((1, DT_PAD)),
            _const_spec((1, SSD_WIDTH)), _const_spec((1, SSD_WIDTH)),
        ],
        out_specs=tok(SSD_WIDTH),
        out_shape=jax.ShapeDtypeStruct((B, L, SSD_WIDTH), BF16),
        scratch_shapes=[
            pltpu.VMEM((tl + SUBLANES, CONV_CH), F32),
            pltpu.VMEM((SSD_GROUPS, (SSD_HEADS // SSD_GROUPS) * SSD_HEAD_DIM, SSD_STATE), F32),
        ],
        compiler_params=_params("arbitrary", "arbitrary"),
        name="ssd_scan",
    )(xbc, dt_raw, z, conv_w, conv_b.reshape(1, CONV_CH), pad_heads(dt_bias), pad_heads(a_log),
      jnp.repeat(d_skip, SSD_HEAD_DIM).reshape(1, SSD_WIDTH), norm_w.reshape(1, SSD_WIDTH))


def _attn_kernel(lam_ref, q_ref, k_ref, v_ref, subw_ref, o_ref, *, seq, tq, lam_init):
    lp = lam_ref[...]
    lam = (jnp.exp(jnp.sum(lp[0:1] * lp[1:2], axis=-1, keepdims=True))
           - jnp.exp(jnp.sum(lp[2:3] * lp[3:4], axis=-1, keepdims=True)) + lam_init)
    lo_half = lax.broadcasted_iota(jnp.int32, (1, LANES), 1) < DIFF_HEAD_DIM
    qchunk = (lax.broadcasted_iota(jnp.int32, (2 * tq, tq), 0) & (tq - 1)) >> 6
    kchunk = lax.broadcasted_iota(jnp.int32, (2 * tq, tq), 1) >> 6
    allowed = kchunk <= qchunk
    for qi in range(seq // tq):
        r0 = qi * tq
        q = q_ref[r0:r0 + tq, :]
        zero = jnp.zeros_like(q)
        q2 = jnp.concatenate([jnp.where(lo_half, q, zero), jnp.where(lo_half, zero, q)], axis=0)
        s_d = jnp.where(allowed, _dot_nt(q2, k_ref[r0:r0 + tq, :]), -jnp.inf)
        m = jnp.max(s_d, axis=-1, keepdims=True)
        if r0 > 0:
            s_o = _dot_nt(q2, k_ref[0:r0, :])
            m = jnp.maximum(m, jnp.max(s_o, axis=-1, keepdims=True))
            e_o = jnp.exp(s_o - m)
            den = jnp.sum(e_o, axis=-1, keepdims=True)
        e_d = jnp.exp(s_d - m)
        den = jnp.sum(e_d, axis=-1, keepdims=True) + (den if r0 > 0 else 0.0)
        inv = 1.0 / den
        w1 = inv[:tq]
        w2 = lam * inv[tq:]
        o = _dot((e_d[:tq] * w1 - e_d[tq:] * w2).astype(BF16), v_ref[r0:r0 + tq, :])
        if r0 > 0:
            o = o + _dot((e_o[:tq] * w1 - e_o[tq:] * w2).astype(BF16), v_ref[0:r0, :])
        o = o * lax.rsqrt(jnp.mean(o * o, axis=-1, keepdims=True) + LN_EPS) * subw_ref[...] * (1.0 - lam_init)
        o_ref[r0:r0 + tq, :] = o.astype(BF16)


def _attention(q, k, v, diff_lambda, subln_w, lam_init):
    B, L, _ = q.shape
    tq = min(L, ATTN_Q_TILE)
    head = pl.BlockSpec((None, L, LANES), lambda b, h: (b, 0, h))
    return pl.pallas_call(
        functools.partial(_attn_kernel, seq=L, tq=tq, lam_init=lam_init),
        grid=(B, DIFF_HEADS),
        in_specs=[_const_spec((4, DIFF_HEAD_DIM)), head, head, head, _const_spec((1, DIFF_V_DIM))],
        out_specs=head,
        out_shape=jax.ShapeDtypeStruct((B, L, DIFF_WIDTH), BF16),
        compiler_params=_params("arbitrary", "arbitrary"),
        name="diff_attention",
    )(diff_lambda, q, k, v, subln_w.reshape(1, DIFF_V_DIM))


def _outproj_kernel(x_ref, y_ref, o_ref, mod_ref, w_ref, lng_ref, lnb_ref, out_ref):
    m = _dot(y_ref[...], w_ref[0:SSD_WIDTH, :]) + _dot(o_ref[...], w_ref[SSD_WIDTH:, :])
    gate = mod_ref[5:6, :]
    t = DEEPNORM_ALPHA * x_ref[...] + (1.0 + gate) * m
    out_ref[...] = _standardize(t) * lng_ref[...] + lnb_ref[...]


def _outproj(x, y, o, mod, w, ln_g, ln_b):
    B, L, _ = x.shape
    tm = min(L, TOKEN_TILE)

    def tok(width):
        return pl.BlockSpec((None, tm, width), lambda b, i: (b, i, 0))

    return pl.pallas_call(
        _outproj_kernel,
        grid=(B, L // tm),
        in_specs=[
            tok(D_MODEL), tok(SSD_WIDTH), tok(DIFF_WIDTH),
            pl.BlockSpec((None, 3 * N_SUB, D_MODEL), lambda b, i: (b, 0, 0)),
            _const_spec((SSD_WIDTH + DIFF_WIDTH, D_MODEL)),
            _const_spec((1, D_MODEL)), _const_spec((1, D_MODEL)),
        ],
        out_specs=tok(D_MODEL),
        out_shape=jax.ShapeDtypeStruct(x.shape, F32),
        compiler_params=_params("arbitrary", "arbitrary"),
        name="mixer_outproj",
    )(x, y, o, mod, w, ln_g, ln_b)


def _pack_w_in(w):
    z, xbc, dt, q, k, v = jnp.split(w, (512, 1536, 1544, 2056, 2568), axis=-1)
    dt = jnp.pad(dt, ((0, 0), (0, DT_PAD - SSD_HEADS)))
    return jnp.concatenate([z, xbc, q, k, v, dt], axis=-1).astype(BF16)


def kernel(x, c, positions, w_ada, b_ada, w_ffn_in, w_ffn_out, w_in, conv_w, conv_b, dt_bias, a_log,
           d_skip, ssd_norm_w, diff_lambda, subln_w, w_out, ln_g, ln_b):
    B = x.shape[0]
    mod_all = _ada_mod(c, w_ada, b_ada).reshape(DEPTH, B, 3 * N_SUB, D_MODEL)
    cos, sin = _rope_tables(positions)
    for layer in range(DEPTH):
        mod = mod_all[layer]
        lam_init = 0.8 - 0.6 * float(np.exp(-0.3 * layer))
        g = ln_g[layer].reshape(N_SUB, 1, D_MODEL)
        b = ln_b[layer].reshape(N_SUB, 1, D_MODEL)
        x = _ffn(x, mod, w_ffn_in[layer, 0].astype(BF16), w_ffn_out[layer, 0].astype(BF16), g[0], b[0], 0)
        z, xbc, q, k, v, dt_raw = _inproj(x, mod, _pack_w_in(w_in[layer]), cos, sin)
        y = _ssd(xbc, dt_raw, z, conv_w[layer], conv_b[layer], dt_bias[layer], a_log[layer],
                 d_skip[layer], ssd_norm_w[layer])
        o = _attention(q, k, v, diff_lambda[layer], subln_w[layer], lam_init)
        x = _outproj(x, y, o, mod, w_out[layer].astype(BF16), g[1], b[1])
        x = _ffn(x, mod, w_ffn_in[layer, 1].astype(BF16), w_ffn_out[layer, 1].astype(BF16), g[2], b[2], 2)
    return x
```

```python
import functools

import numpy as np
import jax
import jax.numpy as jnp
from jax import lax
from jax.experimental import pallas as pl
from jax.experimental.pallas import tpu as pltpu

F32 = jnp.float32
BF16 = jnp.bfloat16

D_MODEL = 1024
DEPTH = 4
CHUNK = 64
SSD_HEADS = 8
SSD_HEAD_DIM = 64
SSD_WIDTH = 512
SSD_GROUPS = 2
SSD_STATE = 128
SSD_CONV = 4
DIFF_HEADS = 4
DIFF_HEAD_DIM = 64
DIFF_V_DIM = 128
DIFF_WIDTH = 512
ROT_DIM = 16
ROPE_THETA = 500000.0
CONV_CH = 1024
QK_WIDTH = 512
D_FF = 2816
N_SUB = 3
LN_EPS = 1e-5
DEEPNORM_ALPHA = (2 * DEPTH) ** 0.25
FFN_RES_WEIGHT = 0.5
LOG2_E = 1.4426950408889634

LANES = 128
SUBLANES = 8
VMEM_LIMIT = 56 * 1024 * 1024

TOKEN_TILE = 512
FFN_CHUNKS = 11
ATTN_Q_TILE = 256
ATTN_KEY_BLOCK = 512
ATTN_ROW_GROUP = 64
SSD_TILE = 256
SSD_Q = 128
DT_PAD = LANES
PROJ_COLS = SSD_WIDTH + CONV_CH + 2 * QK_WIDTH + DIFF_WIDTH + DT_PAD


def _params(*sem):
    return pltpu.CompilerParams(dimension_semantics=sem, vmem_limit_bytes=VMEM_LIMIT)


def _const_spec(shape):
    nd = len(shape)
    return pl.BlockSpec(shape, lambda *_: (0,) * nd, pipeline_mode=pl.Buffered(1))


def _standardize(x):
    mu = jnp.mean(x, axis=-1, keepdims=True)
    xc = x - mu
    var = jnp.mean(xc * xc, axis=-1, keepdims=True)
    return xc * lax.rsqrt(var + LN_EPS)


def _silu(x):
    return x * jax.nn.sigmoid(x)


def _dot(a, b):
    return jnp.dot(a, b, preferred_element_type=F32)


def _dot_nt(a, b):
    return lax.dot_general(a, b, (((1,), (1,)), ((), ())), preferred_element_type=F32)


def _dot_tn(a, b):
    return lax.dot_general(a, b, (((0,), (0,)), ((), ())), preferred_element_type=F32)


def _split3(x):
    hi = x.astype(BF16)
    r1 = x - hi.astype(F32)
    mid = r1.astype(BF16)
    lo = (r1 - mid.astype(F32)).astype(BF16)
    return hi, mid, lo


def _select_exact(x, sel3):
    return _dot(jnp.concatenate(_split3(x), axis=1), sel3)


def _prefix_sum_exact(tri3, x):
    return _dot(tri3, jnp.concatenate(_split3(x), axis=0))


def _ada_kernel(c_ref, w_ref, b_ref, o_ref):
    sc = _silu(c_ref[...]).astype(BF16)
    o_ref[...] = _dot(sc, w_ref[...].astype(BF16)) + b_ref[...]


def _ada_mod(c, w_ada, b_ada):
    B = c.shape[0]
    ncol = w_ada.shape[-1]
    tn = 1024
    return pl.pallas_call(
        _ada_kernel,
        grid=(DEPTH, ncol // tn),
        in_specs=[
            pl.BlockSpec((B, D_MODEL), lambda l, j: (0, 0)),
            pl.BlockSpec((None, D_MODEL, tn), lambda l, j: (l, 0, j)),
            pl.BlockSpec((None, 1, tn), lambda l, j: (l, 0, j)),
        ],
        out_specs=pl.BlockSpec((None, B, tn), lambda l, j: (l, 0, j)),
        out_shape=jax.ShapeDtypeStruct((DEPTH, B, ncol), F32),
        compiler_params=_params("arbitrary", "arbitrary"),
        name="ada_mod",
    )(c, w_ada, b_ada.reshape(DEPTH, 1, ncol))


def _rope_kernel(pos_ref, inv_ref, cos_ref, sin_ref):
    ang = pos_ref[...].astype(F32) * inv_ref[...]
    d = lax.broadcasted_iota(jnp.int32, ang.shape, 1) & (DIFF_HEAD_DIM - 1)
    c = jnp.cos(ang)
    s = jnp.sin(ang)
    cos_ref[...] = jnp.where(d < ROT_DIM, c, 1.0)
    sin_ref[...] = jnp.where(d < ROT_DIM // 2, -s, jnp.where(d < ROT_DIM, s, 0.0))


def _rope_tables(positions):
    B, L = positions.shape
    half = ROT_DIM // 2
    inv = np.float32(ROPE_THETA) ** (-np.arange(half, dtype=np.float32) * np.float32(2.0) / np.float32(ROT_DIM))
    inv_lane = jnp.asarray(np.tile(inv.astype(np.float32), LANES // half)[None, :])
    tm = min(L, 1024)
    spec = pl.BlockSpec((None, tm, LANES), lambda b, i: (b, i, 0))
    return pl.pallas_call(
        _rope_kernel,
        grid=(B, L // tm),
        in_specs=[pl.BlockSpec((None, tm, 1), lambda b, i: (b, i, 0)),
                  pl.BlockSpec((1, LANES), lambda b, i: (0, 0))],
        out_specs=[spec, spec],
        out_shape=[jax.ShapeDtypeStruct((B, L, LANES), F32)] * 2,
        compiler_params=_params("arbitrary", "arbitrary"),
        name="rope_tables",
    )(positions.reshape(B, L, 1), inv_lane)


def _ffn_kernel(x_ref, mod_ref, win_ref, wout_ref, lng_ref, lnb_ref, o_ref, *, sub):
    x = x_ref[...]
    shift = mod_ref[3 * sub:3 * sub + 1, :]
    scale = mod_ref[3 * sub + 1:3 * sub + 2, :]
    gate = mod_ref[3 * sub + 2:3 * sub + 3, :]
    u = (_standardize(x) * (1.0 + scale) + shift).astype(BF16)
    fc = D_FF // FFN_CHUNKS
    f = None
    for k in range(FFN_CHUNKS):
        g = _dot(u, win_ref[:, k * fc:(k + 1) * fc])
        up = _dot(u, win_ref[:, D_FF + k * fc:D_FF + (k + 1) * fc])
        h = (_silu(g) * up).astype(BF16)
        fk = _dot(h, wout_ref[k * fc:(k + 1) * fc, :])
        f = fk if f is None else f + fk
    y = DEEPNORM_ALPHA * x + (FFN_RES_WEIGHT * (1.0 + gate)) * f
    o_ref[...] = _standardize(y) * lng_ref[...] + lnb_ref[...]


def _ffn(x, mod, w_in, w_out, ln_g, ln_b, sub):
    B, L, _ = x.shape
    tm = min(L, TOKEN_TILE)
    xspec = pl.BlockSpec((None, tm, D_MODEL), lambda b, i: (b, i, 0))
    return pl.pallas_call(
        functools.partial(_ffn_kernel, sub=sub),
        grid=(B, L // tm),
        in_specs=[
            xspec,
            pl.BlockSpec((None, 3 * N_SUB, D_MODEL), lambda b, i: (b, 0, 0)),
            _const_spec((D_MODEL, 2 * D_FF)),
            _const_spec((D_FF, D_MODEL)),
            _const_spec((1, D_MODEL)),
            _const_spec((1, D_MODEL)),
        ],
        out_specs=xspec,
        out_shape=jax.ShapeDtypeStruct(x.shape, F32),
        compiler_params=_params("arbitrary", "arbitrary"),
        name=f"ffn{sub}",
    )(x, mod, w_in, w_out, ln_g, ln_b)


_Z0 = 0
_XBC0 = SSD_WIDTH
_Q0 = _XBC0 + CONV_CH
_K0 = _Q0 + QK_WIDTH
_V0 = _K0 + QK_WIDTH
_DT0 = _V0 + DIFF_WIDTH


def _inproj_kernel(x_ref, mod_ref, w_ref, cos_ref, sin_ref,
                   z_ref, xbc_ref, q_ref, k_ref, v_ref, dt_ref):
    x = x_ref[...]
    shift = mod_ref[3:4, :]
    scale = mod_ref[4:5, :]
    u = (_standardize(x) * (1.0 + scale) + shift).astype(BF16)
    z_ref[...] = _dot(u, w_ref[:, _Z0:_XBC0])
    xbc_ref[...] = _dot(u, w_ref[:, _XBC0:_Q0])
    v_ref[...] = _dot(u, w_ref[:, _V0:_DT0]).astype(BF16)
    dt_ref[...] = _dot(u, w_ref[:, _DT0:_DT0 + DT_PAD])
    cos = cos_ref[...]
    sin = sin_ref[...]
    d = lax.broadcasted_iota(jnp.int32, cos.shape, 1) & (DIFF_HEAD_DIM - 1)
    first_half = d < ROT_DIM // 2
    half = ROT_DIM // 2
    q_scale = DIFF_HEAD_DIM ** -0.5 * LOG2_E
    for col0, ref, mult in ((_Q0, q_ref, q_scale), (_K0, k_ref, None)):
        for pair in range(DIFF_HEADS // 2):
            lo2 = pair * 2 * LANES
            t2 = _dot(u, w_ref[:, col0 + lo2:col0 + lo2 + 2 * LANES])
            for j in range(2):
                t = t2[:, j * LANES:(j + 1) * LANES]
                partner = jnp.where(first_half, pltpu.roll(t, LANES - half, axis=1), pltpu.roll(t, half, axis=1))
                r = t * cos + partner * sin
                if mult is not None:
                    r = r * mult
                ref[:, lo2 + j * LANES:lo2 + (j + 1) * LANES] = r.astype(BF16)


def _inproj(x, mod, w, cos, sin):
    B, L, _ = x.shape
    tm = min(L, TOKEN_TILE)

    def tok(width):
        return pl.BlockSpec((None, tm, width), lambda b, i: (b, i, 0))

    return pl.pallas_call(
        _inproj_kernel,
        grid=(B, L // tm),
        in_specs=[
            tok(D_MODEL),
            pl.BlockSpec((None, 3 * N_SUB, D_MODEL), lambda b, i: (b, 0, 0)),
            _const_spec((D_MODEL, PROJ_COLS)),
            tok(LANES), tok(LANES),
        ],
        out_specs=[tok(SSD_WIDTH), tok(CONV_CH), tok(QK_WIDTH), tok(QK_WIDTH), tok(DIFF_WIDTH), tok(DT_PAD)],
        out_shape=[
            jax.ShapeDtypeStruct((B, L, SSD_WIDTH), F32),
            jax.ShapeDtypeStruct((B, L, CONV_CH), F32),
            jax.ShapeDtypeStruct((B, L, QK_WIDTH), BF16),
            jax.ShapeDtypeStruct((B, L, QK_WIDTH), BF16),
            jax.ShapeDtypeStruct((B, L, DIFF_WIDTH), BF16),
            jax.ShapeDtypeStruct((B, L, DT_PAD), F32),
        ],
        compiler_params=_params("arbitrary", "arbitrary"),
        name="mixer_inproj",
    )(x, mod, w, cos, sin)


def _ssd_kernel(xbc_ref, dt_ref, z_ref, convw_ref, convb_ref, dtb_ref, alog_ref, dskip_ref, normw_ref,
                sel3_ref, tri3_ref, y_ref, xpad_ref, state_ref, *, tl):
    halo = SUBLANES
    hg = SSD_HEADS // SSD_GROUPS
    gw = hg * SSD_HEAD_DIM

    @pl.when(pl.program_id(1) == 0)
    def _():
        xpad_ref[0:halo, :] = jnp.zeros((halo, CONV_CH), F32)
        state_ref[...] = jnp.zeros(state_ref.shape, F32)

    xpad_ref[halo:halo + tl, :] = xbc_ref[...]
    xpad = xpad_ref[...]
    conv = convb_ref[...] + convw_ref[SSD_CONV - 1:SSD_CONV, :] * xpad[halo:halo + tl, :]
    for back in range(1, SSD_CONV):
        k = SSD_CONV - 1 - back
        conv = conv + convw_ref[k:k + 1, :] * pltpu.roll(xpad, back, axis=0)[halo:halo + tl, :]
    xpad_ref[0:halo, :] = xpad_ref[tl:tl + halo, :]
    act = _silu(conv)
    xs = act[:, :SSD_WIDTH]
    bm = act[:, SSD_WIDTH:SSD_WIDTH + SSD_GROUPS * SSD_STATE].astype(BF16)
    cm = act[:, SSD_WIDTH + SSD_GROUPS * SSD_STATE:].astype(BF16)

    lane = lax.broadcasted_iota(jnp.int32, (1, LANES), 1)
    v = dt_ref[...] + dtb_ref[...]
    dt = jnp.maximum(v, 0.0) + jnp.log1p(jnp.exp(-jnp.abs(v)))
    a = jnp.where(lane < SSD_HEADS, dt * (-jnp.exp(alog_ref[...])), 0.0)

    li = lax.broadcasted_iota(jnp.int32, (SSD_Q, SSD_Q), 0)
    si = lax.broadcasted_iota(jnp.int32, (SSD_Q, SSD_Q), 1)
    causal = li >= si
    sel3 = sel3_ref[...]
    tri3 = tri3_ref[...]

    xdt = xs * _select_exact(dt, sel3_ref[:, :SSD_WIDTH])
    lo_half = lax.broadcasted_iota(jnp.int32, (1, LANES), 1) < SSD_HEAD_DIM

    y_blocks = []
    for c in range(tl // SSD_Q):
        r0 = c * SSD_Q
        acs = _prefix_sum_exact(tri3, a[r0:r0 + SSD_Q, :])
        acs_t = acs.T
        acs_wide = _select_exact(acs, sel3)
        acs_head = acs_wide[:, :SSD_WIDTH]
        acs_tile = acs_wide[:, SSD_WIDTH:]
        last_head = acs_head[SSD_Q - 1:SSD_Q, :]
        grow = jnp.exp(acs_head)
        to_end = jnp.exp(last_head - acs_head)
        xdt_c = xdt[r0:r0 + SSD_Q, :]
        y_groups = []
        for g in range(SSD_GROUPS):
            bg = bm[r0:r0 + SSD_Q, g * SSD_STATE:(g + 1) * SSD_STATE]
            cg = cm[r0:r0 + SSD_Q, g * SSD_STATE:(g + 1) * SSD_STATE]
            cb = _dot_nt(cg, bg)
            st = state_ref[g]
            y_off = _dot_nt(cg, st.astype(BF16))
            y_pairs = []
            for p in range(hg // 2):
                ms, rhs = [], []
                xp = xdt_c[:, g * gw + p * LANES:g * gw + (p + 1) * LANES].astype(BF16)
                for j in range(2):
                    h = g * hg + 2 * p + j
                    seg = acs_tile[:, h * LANES:(h + 1) * LANES] - acs_t[h:h + 1, :]
                    decay = jnp.exp(jnp.where(causal, seg, -jnp.inf))
                    ms.append((cb * decay).astype(BF16))
                    keep = lo_half if j == 0 else jnp.logical_not(lo_half)
                    rhs.append(jnp.where(keep, xp, jnp.zeros_like(xp)))
                y_pairs.append(_dot(jnp.concatenate(ms, axis=1), jnp.concatenate(rhs, axis=0)))
            y_diag = jnp.concatenate(y_pairs, axis=1)
            y_groups.append(y_diag + y_off * grow[:, g * gw:(g + 1) * gw])
            wgt = (xdt_c[:, g * gw:(g + 1) * gw] * to_end[:, g * gw:(g + 1) * gw]).astype(BF16)
            d_state = _dot_tn(wgt, bg)
            scale_rows = []
            for j in range(hg):
                h = g * hg + j
                scale_rows.append(jnp.broadcast_to(jnp.exp(acs_t[h:h + 1, SSD_Q - 1:SSD_Q]), (SSD_HEAD_DIM, SSD_STATE)))
            state_ref[g] = st * jnp.concatenate(scale_rows, axis=0) + d_state
        y_blocks.append(jnp.concatenate(y_groups, axis=1))
    y = jnp.concatenate(y_blocks, axis=0) if len(y_blocks) > 1 else y_blocks[0]
    y = y + xs * dskip_ref[...]
    y = y * _silu(z_ref[...])
    outs = []
    for g in range(SSD_GROUPS):
        yg = y[:, g * gw:(g + 1) * gw]
        ms = jnp.mean(yg * yg, axis=-1, keepdims=True)
        outs.append(yg * lax.rsqrt(ms + LN_EPS) * normw_ref[:, g * gw:(g + 1) * gw])
    y_ref[...] = jnp.concatenate(outs, axis=1).astype(BF16)


def _ssd_constants():
    wide = SSD_WIDTH + SSD_HEADS * LANES
    row = np.arange(3 * LANES)[:, None] % LANES
    col = np.arange(wide)[None, :]
    col_head = np.where(col < SSD_WIDTH, col // SSD_HEAD_DIM, (col - SSD_WIDTH) // LANES)
    sel3 = (row == col_head).astype(np.float32)
    li = np.arange(SSD_Q)[:, None]
    si = np.arange(3 * SSD_Q)[None, :] % SSD_Q
    tri3 = (li >= si).astype(np.float32)
    return jnp.asarray(sel3, BF16), jnp.asarray(tri3, BF16)


def _ssd(xbc, dt_raw, z, conv_w, conv_b, dt_bias, a_log, d_skip, norm_w):
    B, L, _ = xbc.shape
    tl = min(L, SSD_TILE)
    sel3, tri3 = _ssd_constants()

    def tok(width):
        return pl.BlockSpec((None, tl, width), lambda b, i: (b, i, 0))

    def pad_heads(p):
        return jnp.zeros((1, DT_PAD), F32).at[0, :SSD_HEADS].set(p)

    return pl.pallas_call(
        functools.partial(_ssd_kernel, tl=tl),
        grid=(B, L // tl),
        in_specs=[
            tok(CONV_CH), tok(DT_PAD), tok(SSD_WIDTH),
            _const_spec((SSD_CONV, CONV_CH)), _const_spec((1, CONV_CH)),
            _const_spec((1, DT_PAD)), _const_spec((1, DT_PAD)),
            _const_spec((1, SSD_WIDTH)), _const_spec((1, SSD_WIDTH)),
            _const_spec(sel3.shape), _const_spec(tri3.shape),
        ],
        out_specs=tok(SSD_WIDTH),
        out_shape=jax.ShapeDtypeStruct((B, L, SSD_WIDTH), BF16),
        scratch_shapes=[
            pltpu.VMEM((tl + SUBLANES, CONV_CH), F32),
            pltpu.VMEM((SSD_GROUPS, (SSD_HEADS // SSD_GROUPS) * SSD_HEAD_DIM, SSD_STATE), F32),
        ],
        compiler_params=_params("arbitrary", "arbitrary"),
        name="ssd_scan",
    )(xbc, dt_raw, z, conv_w, conv_b.reshape(1, CONV_CH), pad_heads(dt_bias), pad_heads(a_log),
      jnp.repeat(d_skip, SSD_HEAD_DIM).reshape(1, SSD_WIDTH), norm_w.reshape(1, SSD_WIDTH), sel3, tri3)


def _attn_kernel(lam_ref, q_ref, k_ref, v_ref, subw_ref, o_ref, s_ref, e_ref, vx_ref, *, seq, tq, lam_init):
    rows = 2 * tq
    vx_ref[:, 0:DIFF_V_DIM] = v_ref[...]
    vx_ref[:, DIFF_V_DIM:] = jnp.ones((seq, LANES), BF16)
    lp = lam_ref[...]
    lam = (jnp.exp(jnp.sum(lp[0:1] * lp[1:2], axis=-1, keepdims=True))
           - jnp.exp(jnp.sum(lp[2:3] * lp[3:4], axis=-1, keepdims=True)) + lam_init)
    lo_half = lax.broadcasted_iota(jnp.int32, (1, LANES), 1) < DIFF_HEAD_DIM
    qchunk = (lax.broadcasted_iota(jnp.int32, (rows, tq), 0) & (tq - 1)) >> 6
    kchunk = lax.broadcasted_iota(jnp.int32, (rows, tq), 1) >> 6
    allowed = kchunk <= qchunk
    for qi in reversed(range(seq // tq)):
        r0 = qi * tq
        ext = r0 + tq
        q = q_ref[r0:r0 + tq, :]
        zero = jnp.zeros_like(q)
        q2 = jnp.concatenate([jnp.where(lo_half, q, zero), jnp.where(lo_half, zero, q)], axis=0)
        blocks = [(c0, min(ATTN_KEY_BLOCK, r0 - c0)) for c0 in range(0, r0, ATTN_KEY_BLOCK)] + [(r0, tq)]
        for c0, cw in blocks:
            sc = _dot_nt(q2, k_ref[c0:c0 + cw, :])
            if c0 == r0:
                sc = jnp.where(allowed, sc, -jnp.inf)
            s_ref[:, c0:c0 + cw] = sc
        for rg in range(rows // ATTN_ROW_GROUP):
            rsl = slice(rg * ATTN_ROW_GROUP, (rg + 1) * ATTN_ROW_GROUP)
            mx = s_ref[rsl, 0:LANES]
            for j in range(1, ext // LANES):
                mx = jnp.maximum(mx, s_ref[rsl, j * LANES:(j + 1) * LANES])
            mb = jnp.broadcast_to(jnp.max(mx, axis=-1, keepdims=True), (ATTN_ROW_GROUP, LANES))
            for j in range(ext // LANES):
                csl = slice(j * LANES, (j + 1) * LANES)
                e_ref[rsl, csl] = jnp.exp2(s_ref[rsl, csl] - mb).astype(BF16)
        acc = _dot(e_ref[:, 0:ext], vx_ref[0:ext, :])
        inv = 1.0 / acc[:, DIFF_V_DIM:]
        acc = acc[:, :DIFF_V_DIM]
        o = acc[:tq] * inv[:tq] - acc[tq:] * (lam * inv[tq:])
        o = o * lax.rsqrt(jnp.mean(o * o, axis=-1, keepdims=True) + LN_EPS) * subw_ref[...] * (1.0 - lam_init)
        o_ref[r0:r0 + tq, :] = o.astype(BF16)


def _attention(q, k, v, diff_lambda, subln_w, lam_init):
    B, L, _ = q.shape
    tq = min(L, ATTN_Q_TILE)
    head = pl.BlockSpec((None, L, LANES), lambda b, h: (b, 0, h))
    return pl.pallas_call(
        functools.partial(_attn_kernel, seq=L, tq=tq, lam_init=lam_init),
        grid=(B, DIFF_HEADS),
        in_specs=[_const_spec((4, DIFF_HEAD_DIM)), head, head, head, _const_spec((1, DIFF_V_DIM))],
        out_specs=head,
        out_shape=jax.ShapeDtypeStruct((B, L, DIFF_WIDTH), BF16),
        scratch_shapes=[
            pltpu.VMEM((2 * tq, L), F32),
            pltpu.VMEM((2 * tq, L), BF16),
            pltpu.VMEM((L, DIFF_V_DIM + LANES), BF16),
        ],
        compiler_params=_params("arbitrary", "arbitrary"),
        name="diff_attention",
    )(diff_lambda, q, k, v, subln_w.reshape(1, DIFF_V_DIM))


def _outproj_kernel(x_ref, y_ref, o_ref, mod_ref, w_ref, lng_ref, lnb_ref, out_ref):
    m = _dot(y_ref[...], w_ref[0:SSD_WIDTH, :]) + _dot(o_ref[...], w_ref[SSD_WIDTH:, :])
    gate = mod_ref[5:6, :]
    t = DEEPNORM_ALPHA * x_ref[...] + (1.0 + gate) * m
    out_ref[...] = _standardize(t) * lng_ref[...] + lnb_ref[...]


def _outproj(x, y, o, mod, w, ln_g, ln_b):
    B, L, _ = x.shape
    tm = min(L, TOKEN_TILE)

    def tok(width):
        return pl.BlockSpec((None, tm, width), lambda b, i: (b, i, 0))

    return pl.pallas_call(
        _outproj_kernel,
        grid=(B, L // tm),
        in_specs=[
            tok(D_MODEL), tok(SSD_WIDTH), tok(DIFF_WIDTH),
            pl.BlockSpec((None, 3 * N_SUB, D_MODEL), lambda b, i: (b, 0, 0)),
            _const_spec((SSD_WIDTH + DIFF_WIDTH, D_MODEL)),
            _const_spec((1, D_MODEL)), _const_spec((1, D_MODEL)),
        ],
        out_specs=tok(D_MODEL),
        out_shape=jax.ShapeDtypeStruct(x.shape, F32),
        compiler_params=_params("arbitrary", "arbitrary"),
        name="mixer_outproj",
    )(x, y, o, mod, w, ln_g, ln_b)


def _pack_w_in(w):
    z, xbc, dt, q, k, v = jnp.split(w, (512, 1536, 1544, 2056, 2568), axis=-1)
    dt = jnp.pad(dt, ((0, 0), (0, DT_PAD - SSD_HEADS)))
    return jnp.concatenate([z, xbc, q, k, v, dt], axis=-1).astype(BF16)


def kernel(x, c, positions, w_ada, b_ada, w_ffn_in, w_ffn_out, w_in, conv_w, conv_b, dt_bias, a_log,
           d_skip, ssd_norm_w, diff_lambda, subln_w, w_out, ln_g, ln_b):
    B = x.shape[0]
    mod_all = _ada_mod(c, w_ada, b_ada).reshape(DEPTH, B, 3 * N_SUB, D_MODEL)
    cos, sin = _rope_tables(positions)
    for layer in range(DEPTH):
        mod = mod_all[layer]
        lam_init = 0.8 - 0.6 * float(np.exp(-0.3 * layer))
        g = ln_g[layer].reshape(N_SUB, 1, D_MODEL)
        b = ln_b[layer].reshape(N_SUB, 1, D_MODEL)
        x = _ffn(x, mod, w_ffn_in[layer, 0].astype(BF16), w_ffn_out[layer, 0].astype(BF16), g[0], b[0], 0)
        z, xbc, q, k, v, dt_raw = _inproj(x, mod, _pack_w_in(w_in[layer]), cos, sin)
        y = _ssd(xbc, dt_raw, z, conv_w[layer], conv_b[layer], dt_bias[layer], a_log[layer],
                 d_skip[layer], ssd_norm_w[layer])
        o = _attention(q, k, v, diff_lambda[layer], subln_w[layer], lam_init)
        x = _outproj(x, y, o, mod, w_out[layer].astype(BF16), g[1], b[1])
        x = _ffn(x, mod, w_ffn_in[layer, 1].astype(BF16), w_ffn_out[layer, 1].astype(BF16), g[2], b[2], 2)
    return x
```

```python
import functools

import numpy as np
import jax
import jax.numpy as jnp
from jax import lax
from jax.experimental import pallas as pl
from jax.experimental.pallas import tpu as pltpu

F32 = jnp.float32
BF16 = jnp.bfloat16

D_MODEL = 1024
DEPTH = 4
CHUNK = 64
SSD_HEADS = 8
SSD_HEAD_DIM = 64
SSD_WIDTH = 512
SSD_GROUPS = 2
SSD_STATE = 128
SSD_CONV = 4
DIFF_HEADS = 4
DIFF_HEAD_DIM = 64
DIFF_V_DIM = 128
DIFF_WIDTH = 512
ROT_DIM = 16
ROPE_THETA = 500000.0
CONV_CH = 1024
QK_WIDTH = 512
D_FF = 2816
N_SUB = 3
LN_EPS = 1e-5
DEEPNORM_ALPHA = (2 * DEPTH) ** 0.25
FFN_RES_WEIGHT = 0.5
LOG2_E = 1.4426950408889634

LANES = 128
SUBLANES = 8
VMEM_LIMIT = 56 * 1024 * 1024

TOKEN_TILE = 512
FFN_TOKEN_TILE = 1024
FFN_SUBTILE = 512
FFN_PIECE = 64
FFN_CHUNKS = 11
ATTN_Q_TILE = 256
ATTN_KEY_BLOCK = 512
ATTN_ROW_GROUP = 64
SSD_TILE = 256
SSD_Q = 128
DT_PAD = LANES
PROJ_COLS = SSD_WIDTH + CONV_CH + 2 * QK_WIDTH + DIFF_WIDTH + DT_PAD


def _params(*sem):
    return pltpu.CompilerParams(dimension_semantics=sem, vmem_limit_bytes=VMEM_LIMIT)


def _const_spec(shape):
    nd = len(shape)
    return pl.BlockSpec(shape, lambda *_: (0,) * nd, pipeline_mode=pl.Buffered(1))


def _standardize(x):
    mu = jnp.mean(x, axis=-1, keepdims=True)
    xc = x - mu
    var = jnp.mean(xc * xc, axis=-1, keepdims=True)
    return xc * lax.rsqrt(var + LN_EPS)


def _silu(x):
    return x * jax.nn.sigmoid(x)


def _dot(a, b):
    return jnp.dot(a, b, preferred_element_type=F32)


def _dot_nt(a, b):
    return lax.dot_general(a, b, (((1,), (1,)), ((), ())), preferred_element_type=F32)


def _dot_tn(a, b):
    return lax.dot_general(a, b, (((0,), (0,)), ((), ())), preferred_element_type=F32)


def _split3(x):
    hi = x.astype(BF16)
    r1 = x - hi.astype(F32)
    mid = r1.astype(BF16)
    lo = (r1 - mid.astype(F32)).astype(BF16)
    return hi, mid, lo


def _select_exact(x, sel3):
    return _dot(jnp.concatenate(_split3(x), axis=1), sel3)


def _prefix_sum_exact(tri3, x):
    return _dot(tri3, jnp.concatenate(_split3(x), axis=0))


def _ada_kernel(c_ref, w_ref, b_ref, o_ref):
    sc = _silu(c_ref[...]).astype(BF16)
    o_ref[...] = _dot(sc, w_ref[...].astype(BF16)) + b_ref[...]


def _ada_mod(c, w_ada, b_ada):
    B = c.shape[0]
    ncol = w_ada.shape[-1]
    tn = 1024
    return pl.pallas_call(
        _ada_kernel,
        grid=(DEPTH, ncol // tn),
        in_specs=[
            pl.BlockSpec((B, D_MODEL), lambda l, j: (0, 0)),
            pl.BlockSpec((None, D_MODEL, tn), lambda l, j: (l, 0, j)),
            pl.BlockSpec((None, 1, tn), lambda l, j: (l, 0, j)),
        ],
        out_specs=pl.BlockSpec((None, B, tn), lambda l, j: (l, 0, j)),
        out_shape=jax.ShapeDtypeStruct((DEPTH, B, ncol), F32),
        compiler_params=_params("arbitrary", "arbitrary"),
        name="ada_mod",
    )(c, w_ada, b_ada.reshape(DEPTH, 1, ncol))


def _rope_kernel(pos_ref, inv_ref, cos_ref, sin_ref):
    ang = pos_ref[...].astype(F32) * inv_ref[...]
    d = lax.broadcasted_iota(jnp.int32, ang.shape, 1) & (DIFF_HEAD_DIM - 1)
    c = jnp.cos(ang)
    s = jnp.sin(ang)
    cos_ref[...] = jnp.where(d < ROT_DIM, c, 1.0)
    sin_ref[...] = jnp.where(d < ROT_DIM // 2, -s, jnp.where(d < ROT_DIM, s, 0.0))


def _rope_tables(positions):
    B, L = positions.shape
    half = ROT_DIM // 2
    inv = np.float32(ROPE_THETA) ** (-np.arange(half, dtype=np.float32) * np.float32(2.0) / np.float32(ROT_DIM))
    inv_lane = jnp.asarray(np.tile(inv.astype(np.float32), LANES // half)[None, :])
    tm = min(L, 1024)
    spec = pl.BlockSpec((None, tm, LANES), lambda b, i: (b, i, 0))
    return pl.pallas_call(
        _rope_kernel,
        grid=(B, L // tm),
        in_specs=[pl.BlockSpec((None, tm, 1), lambda b, i: (b, i, 0)),
                  pl.BlockSpec((1, LANES), lambda b, i: (0, 0))],
        out_specs=[spec, spec],
        out_shape=[jax.ShapeDtypeStruct((B, L, LANES), F32)] * 2,
        compiler_params=_params("arbitrary", "arbitrary"),
        name="rope_tables",
    )(positions.reshape(B, L, 1), inv_lane)


def _zero_after(x, width):
    bits = pltpu.bitcast(x, jnp.int32)
    acc = bits[0:SUBLANES, :]
    for r in range(SUBLANES, x.shape[0], SUBLANES):
        acc = acc | bits[r:r + SUBLANES, :]
    lane = acc[:, 0:LANES]
    for c in range(LANES, x.shape[1], LANES):
        lane = lane | acc[:, c:c + LANES]
    sixteen = jnp.full(lane.shape, 16, jnp.int32)
    zero = lax.shift_right_logical(lax.shift_right_logical(lane, sixteen), sixteen).astype(F32)
    return jnp.concatenate([zero] * (width // LANES), axis=1)


def _ffn_block(n_rows, sub, stream_piece, next_tile_matmul, mod_ref, win_ref, wout_ref, lng_ref, lnb_ref,
               o_ref, u_ref, f_ref):
    shift = mod_ref[3 * sub:3 * sub + 1, :]
    scale1 = 1.0 + mod_ref[3 * sub + 1:3 * sub + 2, :]
    res_gain = FFN_RES_WEIGHT * (1.0 + mod_ref[3 * sub + 2:3 * sub + 3, :])
    tiles = n_rows // FFN_SUBTILE
    pieces = FFN_SUBTILE // FFN_PIECE
    fc = D_FF // FFN_CHUNKS
    assert pieces <= FFN_CHUNKS

    def tile_rows(t):
        return slice(t * FFN_SUBTILE, (t + 1) * FFN_SUBTILE)

    def modulate_piece(t, p):
        rows = slice(t * FFN_SUBTILE + p * FFN_PIECE, t * FFN_SUBTILE + (p + 1) * FFN_PIECE)
        s = stream_piece(rows)
        o_ref[rows, :] = s
        u = _standardize(s) * scale1 + shift
        u_ref[t % 2, p * FFN_PIECE:(p + 1) * FFN_PIECE, :] = u.astype(BF16)
        return u

    def norm_piece(t, p):
        rows = slice(t * FFN_SUBTILE + p * FFN_PIECE, t * FFN_SUBTILE + (p + 1) * FFN_PIECE)
        y = DEEPNORM_ALPHA * o_ref[rows, :] + res_gain * f_ref[p * FFN_PIECE:(p + 1) * FFN_PIECE, :]
        o_ref[rows, :] = _standardize(y) * lng_ref[...] + lnb_ref[...]

    if next_tile_matmul is not None:
        next_tile_matmul(tile_rows(0))
    for p in range(pieces):
        modulate_piece(0, p)
    for t in range(tiles):
        if next_tile_matmul is not None and t + 1 < tiles:
            next_tile_matmul(tile_rows(t + 1))
        f = None
        pending = None
        for k in range(FFN_CHUNKS):
            u = u_ref[t % 2]
            g = _dot(u, win_ref[:, k * fc:(k + 1) * fc])
            up = _dot(u, win_ref[:, D_FF + k * fc:D_FF + (k + 1) * fc])
            if pending is not None:
                up = jnp.concatenate([up[0:SUBLANES] + _zero_after(pending, fc), up[SUBLANES:]], axis=0)
                pending = None
            h = (_silu(g) * up).astype(BF16)
            fk = _dot(h, wout_ref[k * fc:(k + 1) * fc, :])
            f = fk if f is None else f + fk
            if k < pieces:
                if t + 1 < tiles:
                    pending = modulate_piece(t + 1, k)
                if t >= 1:
                    norm_piece(t - 1, k)
        f_ref[...] = f
    for p in range(pieces):
        norm_piece(tiles - 1, p)


def _ffn_kernel(x_ref, mod_ref, win_ref, wout_ref, lng_ref, lnb_ref, o_ref, u_ref, f_ref, *, sub):
    _ffn_block(x_ref.shape[0], sub, lambda rows: x_ref[rows, :], None,
               mod_ref, win_ref, wout_ref, lng_ref, lnb_ref, o_ref, u_ref, f_ref)


def _mixer_out_ffn_kernel(x_ref, y_ref, a_ref, mod_ref, wmix_ref, lng1_ref, lnb1_ref,
                          win_ref, wout_ref, lng_ref, lnb_ref, o_ref, u_ref, f_ref, m_ref):
    gain = 1.0 + mod_ref[5:6, :]

    def mixer_matmul(rows):
        m_ref[...] = _dot(y_ref[rows, :], wmix_ref[0:SSD_WIDTH, :]) + _dot(a_ref[rows, :], wmix_ref[SSD_WIDTH:, :])

    def stream_piece(rows):
        local = slice(rows.start % FFN_SUBTILE, rows.start % FFN_SUBTILE + FFN_PIECE)
        t = DEEPNORM_ALPHA * x_ref[rows, :] + gain * m_ref[local, :]
        return _standardize(t) * lng1_ref[...] + lnb1_ref[...]

    _ffn_block(x_ref.shape[0], 2, stream_piece, mixer_matmul,
               mod_ref, win_ref, wout_ref, lng_ref, lnb_ref, o_ref, u_ref, f_ref)


def _ffn_scratch():
    return [pltpu.VMEM((2, FFN_SUBTILE, D_MODEL), BF16), pltpu.VMEM((FFN_SUBTILE, D_MODEL), F32)]


def _ffn_specs(B, L):
    tm = min(L, FFN_TOKEN_TILE)

    def tok(width):
        return pl.BlockSpec((None, tm, width), lambda b, i: (b, i, 0))

    mod_spec = pl.BlockSpec((None, 3 * N_SUB, D_MODEL), lambda b, i: (b, 0, 0))
    weights = [_const_spec((D_MODEL, 2 * D_FF)), _const_spec((D_FF, D_MODEL)),
               _const_spec((1, D_MODEL)), _const_spec((1, D_MODEL))]
    return (B, L // tm), tok, mod_spec, weights


def _ffn(x, mod, w_in, w_out, ln_g, ln_b, sub):
    B, L, _ = x.shape
    grid, tok, mod_spec, weights = _ffn_specs(B, L)
    return pl.pallas_call(
        functools.partial(_ffn_kernel, sub=sub),
        grid=grid,
        in_specs=[tok(D_MODEL), mod_spec] + weights,
        out_specs=tok(D_MODEL),
        out_shape=jax.ShapeDtypeStruct(x.shape, F32),
        scratch_shapes=_ffn_scratch(),
        compiler_params=_params("arbitrary", "arbitrary"),
        name=f"ffn{sub}",
    )(x, mod, w_in, w_out, ln_g, ln_b)


def _mixer_out_ffn(x, y, a, mod, w_mix, ln_g1, ln_b1, w_in, w_out, ln_g, ln_b):
    B, L, _ = x.shape
    grid, tok, mod_spec, weights = _ffn_specs(B, L)
    return pl.pallas_call(
        _mixer_out_ffn_kernel,
        grid=grid,
        in_specs=[tok(D_MODEL), tok(SSD_WIDTH), tok(DIFF_WIDTH), mod_spec,
                  _const_spec((SSD_WIDTH + DIFF_WIDTH, D_MODEL)), _const_spec((1, D_MODEL)), _const_spec((1, D_MODEL))]
        + weights,
        out_specs=tok(D_MODEL),
        out_shape=jax.ShapeDtypeStruct(x.shape, F32),
        scratch_shapes=_ffn_scratch() + [pltpu.VMEM((FFN_SUBTILE, D_MODEL), F32)],
        compiler_params=_params("arbitrary", "arbitrary"),
        name="mixer_out_ffn2",
    )(x, y, a, mod, w_mix, ln_g1, ln_b1, w_in, w_out, ln_g, ln_b)


_Z0 = 0
_XBC0 = SSD_WIDTH
_Q0 = _XBC0 + CONV_CH
_K0 = _Q0 + QK_WIDTH
_V0 = _K0 + QK_WIDTH
_DT0 = _V0 + DIFF_WIDTH


def _inproj_kernel(x_ref, mod_ref, w_ref, cos_ref, sin_ref,
                   z_ref, xbc_ref, q_ref, k_ref, v_ref, dt_ref):
    x = x_ref[...]
    shift = mod_ref[3:4, :]
    scale = mod_ref[4:5, :]
    u = (_standardize(x) * (1.0 + scale) + shift).astype(BF16)
    z_ref[...] = _dot(u, w_ref[:, _Z0:_XBC0])
    xbc_ref[...] = _dot(u, w_ref[:, _XBC0:_Q0])
    v_ref[...] = _dot(u, w_ref[:, _V0:_DT0]).astype(BF16)
    dt_ref[...] = _dot(u, w_ref[:, _DT0:_DT0 + DT_PAD])
    cos = cos_ref[...]
    sin = sin_ref[...]
    d = lax.broadcasted_iota(jnp.int32, cos.shape, 1) & (DIFF_HEAD_DIM - 1)
    first_half = d < ROT_DIM // 2
    half = ROT_DIM // 2
    q_scale = DIFF_HEAD_DIM ** -0.5 * LOG2_E
    for col0, ref, mult in ((_Q0, q_ref, q_scale), (_K0, k_ref, None)):
        for pair in range(DIFF_HEADS // 2):
            lo2 = pair * 2 * LANES
            t2 = _dot(u, w_ref[:, col0 + lo2:col0 + lo2 + 2 * LANES])
            for j in range(2):
                t = t2[:, j * LANES:(j + 1) * LANES]
                partner = jnp.where(first_half, pltpu.roll(t, LANES - half, axis=1), pltpu.roll(t, half, axis=1))
                r = t * cos + partner * sin
                if mult is not None:
                    r = r * mult
                ref[:, lo2 + j * LANES:lo2 + (j + 1) * LANES] = r.astype(BF16)


def _inproj(x, mod, w, cos, sin):
    B, L, _ = x.shape
    tm = min(L, TOKEN_TILE)

    def tok(width):
        return pl.BlockSpec((None, tm, width), lambda b, i: (b, i, 0))

    return pl.pallas_call(
        _inproj_kernel,
        grid=(B, L // tm),
        in_specs=[
            tok(D_MODEL),
            pl.BlockSpec((None, 3 * N_SUB, D_MODEL), lambda b, i: (b, 0, 0)),
            _const_spec((D_MODEL, PROJ_COLS)),
            tok(LANES), tok(LANES),
        ],
        out_specs=[tok(SSD_WIDTH), tok(CONV_CH), tok(QK_WIDTH), tok(QK_WIDTH), tok(DIFF_WIDTH), tok(DT_PAD)],
        out_shape=[
            jax.ShapeDtypeStruct((B, L, SSD_WIDTH), F32),
            jax.ShapeDtypeStruct((B, L, CONV_CH), F32),
            jax.ShapeDtypeStruct((B, L, QK_WIDTH), BF16),
            jax.ShapeDtypeStruct((B, L, QK_WIDTH), BF16),
            jax.ShapeDtypeStruct((B, L, DIFF_WIDTH), BF16),
            jax.ShapeDtypeStruct((B, L, DT_PAD), F32),
        ],
        compiler_params=_params("arbitrary", "arbitrary"),
        name="mixer_inproj",
    )(x, mod, w, cos, sin)


def _ssd_kernel(xbc_ref, dt_ref, z_ref, convw_ref, convb_ref, dtb_ref, alog_ref, dskip_ref, normw_ref,
                sel3_ref, tri3_ref, y_ref, xpad_ref, state_ref, *, tl):
    halo = SUBLANES
    hg = SSD_HEADS // SSD_GROUPS
    gw = hg * SSD_HEAD_DIM

    @pl.when(pl.program_id(1) == 0)
    def _():
        xpad_ref[0:halo, :] = jnp.zeros((halo, CONV_CH), F32)
        state_ref[...] = jnp.zeros(state_ref.shape, F32)

    xpad_ref[halo:halo + tl, :] = xbc_ref[...]
    xpad = xpad_ref[...]
    conv = convb_ref[...] + convw_ref[SSD_CONV - 1:SSD_CONV, :] * xpad[halo:halo + tl, :]
    for back in range(1, SSD_CONV):
        k = SSD_CONV - 1 - back
        conv = conv + convw_ref[k:k + 1, :] * pltpu.roll(xpad, back, axis=0)[halo:halo + tl, :]
    xpad_ref[0:halo, :] = xpad_ref[tl:tl + halo, :]
    act = _silu(conv)
    xs = act[:, :SSD_WIDTH]
    bm = act[:, SSD_WIDTH:SSD_WIDTH + SSD_GROUPS * SSD_STATE].astype(BF16)
    cm = act[:, SSD_WIDTH + SSD_GROUPS * SSD_STATE:].astype(BF16)

    lane = lax.broadcasted_iota(jnp.int32, (1, LANES), 1)
    v = dt_ref[...] + dtb_ref[...]
    dt = jnp.maximum(v, 0.0) + jnp.log1p(jnp.exp(-jnp.abs(v)))
    a = jnp.where(lane < SSD_HEADS, dt * (-jnp.exp(alog_ref[...])), 0.0)

    li = lax.broadcasted_iota(jnp.int32, (SSD_Q, SSD_Q), 0)
    si = lax.broadcasted_iota(jnp.int32, (SSD_Q, SSD_Q), 1)
    causal = li >= si
    sel3 = sel3_ref[...]
    tri3 = tri3_ref[...]

    xdt = xs * _select_exact(dt, sel3_ref[:, :SSD_WIDTH])
    lo_half = lax.broadcasted_iota(jnp.int32, (1, LANES), 1) < SSD_HEAD_DIM

    y_blocks = []
    for c in range(tl // SSD_Q):
        r0 = c * SSD_Q
        acs = _prefix_sum_exact(tri3, a[r0:r0 + SSD_Q, :])
        acs_t = acs.T
        acs_wide = _select_exact(acs, sel3)
        acs_head = acs_wide[:, :SSD_WIDTH]
        acs_tile = acs_wide[:, SSD_WIDTH:]
        last_head = acs_head[SSD_Q - 1:SSD_Q, :]
        grow = jnp.exp(acs_head)
        to_end = jnp.exp(last_head - acs_head)
        xdt_c = xdt[r0:r0 + SSD_Q, :]
        y_groups = []
        for g in range(SSD_GROUPS):
            bg = bm[r0:r0 + SSD_Q, g * SSD_STATE:(g + 1) * SSD_STATE]
            cg = cm[r0:r0 + SSD_Q, g * SSD_STATE:(g + 1) * SSD_STATE]
            cb = _dot_nt(cg, bg)
            st = state_ref[g]
            y_off = _dot_nt(cg, st.astype(BF16))
            y_pairs = []
            for p in range(hg // 2):
                ms, rhs = [], []
                xp = xdt_c[:, g * gw + p * LANES:g * gw + (p + 1) * LANES].astype(BF16)
                for j in range(2):
                    h = g * hg + 2 * p + j
                    seg = acs_tile[:, h * LANES:(h + 1) * LANES] - acs_t[h:h + 1, :]
                    decay = jnp.exp(jnp.where(causal, seg, -jnp.inf))
                    ms.append((cb * decay).astype(BF16))
                    keep = lo_half if j == 0 else jnp.logical_not(lo_half)
                    rhs.append(jnp.where(keep, xp, jnp.zeros_like(xp)))
                y_pairs.append(_dot(jnp.concatenate(ms, axis=1), jnp.concatenate(rhs, axis=0)))
            y_diag = jnp.concatenate(y_pairs, axis=1)
            y_groups.append(y_diag + y_off * grow[:, g * gw:(g + 1) * gw])
            wgt = (xdt_c[:, g * gw:(g + 1) * gw] * to_end[:, g * gw:(g + 1) * gw]).astype(BF16)
            d_state = _dot_tn(wgt, bg)
            scale_rows = []
            for j in range(hg):
                h = g * hg + j
                scale_rows.append(jnp.broadcast_to(jnp.exp(acs_t[h:h + 1, SSD_Q - 1:SSD_Q]), (SSD_HEAD_DIM, SSD_STATE)))
            state_ref[g] = st * jnp.concatenate(scale_rows, axis=0) + d_state
        y_blocks.append(jnp.concatenate(y_groups, axis=1))
    y = jnp.concatenate(y_blocks, axis=0) if len(y_blocks) > 1 else y_blocks[0]
    y = y + xs * dskip_ref[...]
    y = y * _silu(z_ref[...])
    outs = []
    for g in range(SSD_GROUPS):
        yg = y[:, g * gw:(g + 1) * gw]
        ms = jnp.mean(yg * yg, axis=-1, keepdims=True)
        outs.append(yg * lax.rsqrt(ms + LN_EPS) * normw_ref[:, g * gw:(g + 1) * gw])
    y_ref[...] = jnp.concatenate(outs, axis=1).astype(BF16)


def _ssd_constants():
    wide = SSD_WIDTH + SSD_HEADS * LANES
    row = np.arange(3 * LANES)[:, None] % LANES
    col = np.arange(wide)[None, :]
    col_head = np.where(col < SSD_WIDTH, col // SSD_HEAD_DIM, (col - SSD_WIDTH) // LANES)
    sel3 = (row == col_head).astype(np.float32)
    li = np.arange(SSD_Q)[:, None]
    si = np.arange(3 * SSD_Q)[None, :] % SSD_Q
    tri3 = (li >= si).astype(np.float32)
    return jnp.asarray(sel3, BF16), jnp.asarray(tri3, BF16)


def _ssd(xbc, dt_raw, z, conv_w, conv_b, dt_bias, a_log, d_skip, norm_w):
    B, L, _ = xbc.shape
    tl = min(L, SSD_TILE)
    sel3, tri3 = _ssd_constants()

    def tok(width):
        return pl.BlockSpec((None, tl, width), lambda b, i: (b, i, 0))

    def pad_heads(p):
        return jnp.zeros((1, DT_PAD), F32).at[0, :SSD_HEADS].set(p)

    return pl.pallas_call(
        functools.partial(_ssd_kernel, tl=tl),
        grid=(B, L // tl),
        in_specs=[
            tok(CONV_CH), tok(DT_PAD), tok(SSD_WIDTH),
            _const_spec((SSD_CONV, CONV_CH)), _const_spec((1, CONV_CH)),
            _const_spec((1, DT_PAD)), _const_spec((1, DT_PAD)),
            _const_spec((1, SSD_WIDTH)), _const_spec((1, SSD_WIDTH)),
            _const_spec(sel3.shape), _const_spec(tri3.shape),
        ],
        out_specs=tok(SSD_WIDTH),
        out_shape=jax.ShapeDtypeStruct((B, L, SSD_WIDTH), BF16),
        scratch_shapes=[
            pltpu.VMEM((tl + SUBLANES, CONV_CH), F32),
            pltpu.VMEM((SSD_GROUPS, (SSD_HEADS // SSD_GROUPS) * SSD_HEAD_DIM, SSD_STATE), F32),
        ],
        compiler_params=_params("arbitrary", "arbitrary"),
        name="ssd_scan",
    )(xbc, dt_raw, z, conv_w, conv_b.reshape(1, CONV_CH), pad_heads(dt_bias), pad_heads(a_log),
      jnp.repeat(d_skip, SSD_HEAD_DIM).reshape(1, SSD_WIDTH), norm_w.reshape(1, SSD_WIDTH), sel3, tri3)


def _attn_kernel(lam_ref, q_ref, k_ref, v_ref, subw_ref, o_ref, s_ref, e_ref, vx_ref, *, seq, tq, lam_init):
    rows = 2 * tq
    vx_ref[:, 0:DIFF_V_DIM] = v_ref[...]
    vx_ref[:, DIFF_V_DIM:] = jnp.ones((seq, LANES), BF16)
    lp = lam_ref[...]
    lam = (jnp.exp(jnp.sum(lp[0:1] * lp[1:2], axis=-1, keepdims=True))
           - jnp.exp(jnp.sum(lp[2:3] * lp[3:4], axis=-1, keepdims=True)) + lam_init)
    lo_half = lax.broadcasted_iota(jnp.int32, (1, LANES), 1) < DIFF_HEAD_DIM
    qchunk = (lax.broadcasted_iota(jnp.int32, (rows, tq), 0) & (tq - 1)) >> 6
    kchunk = lax.broadcasted_iota(jnp.int32, (rows, tq), 1) >> 6
    allowed = kchunk <= qchunk
    for qi in reversed(range(seq // tq)):
        r0 = qi * tq
        ext = r0 + tq
        q = q_ref[r0:r0 + tq, :]
        zero = jnp.zeros_like(q)
        q2 = jnp.concatenate([jnp.where(lo_half, q, zero), jnp.where(lo_half, zero, q)], axis=0)
        blocks = [(c0, min(ATTN_KEY_BLOCK, r0 - c0)) for c0 in range(0, r0, ATTN_KEY_BLOCK)] + [(r0, tq)]
        for c0, cw in blocks:
            sc = _dot_nt(q2, k_ref[c0:c0 + cw, :])
            if c0 == r0:
                sc = jnp.where(allowed, sc, -jnp.inf)
            s_ref[:, c0:c0 + cw] = sc
        for rg in range(rows // ATTN_ROW_GROUP):
            rsl = slice(rg * ATTN_ROW_GROUP, (rg + 1) * ATTN_ROW_GROUP)
            mx = s_ref[rsl, 0:LANES]
            for j in range(1, ext // LANES):
                mx = jnp.maximum(mx, s_ref[rsl, j * LANES:(j + 1) * LANES])
            mb = jnp.broadcast_to(jnp.max(mx, axis=-1, keepdims=True), (ATTN_ROW_GROUP, LANES))
            for j in range(ext // LANES):
                csl = slice(j * LANES, (j + 1) * LANES)
                e_ref[rsl, csl] = jnp.exp2(s_ref[rsl, csl] - mb).astype(BF16)
        acc = _dot(e_ref[:, 0:ext], vx_ref[0:ext, :])
        inv = 1.0 / acc[:, DIFF_V_DIM:]
        acc = acc[:, :DIFF_V_DIM]
        o = acc[:tq] * inv[:tq] - acc[tq:] * (lam * inv[tq:])
        o = o * lax.rsqrt(jnp.mean(o * o, axis=-1, keepdims=True) + LN_EPS) * subw_ref[...] * (1.0 - lam_init)
        o_ref[r0:r0 + tq, :] = o.astype(BF16)


def _attention(q, k, v, diff_lambda, subln_w, lam_init):
    B, L, _ = q.shape
    tq = min(L, ATTN_Q_TILE)
    head = pl.BlockSpec((None, L, LANES), lambda b, h: (b, 0, h))
    return pl.pallas_call(
        functools.partial(_attn_kernel, seq=L, tq=tq, lam_init=lam_init),
        grid=(B, DIFF_HEADS),
        in_specs=[_const_spec((4, DIFF_HEAD_DIM)), head, head, head, _const_spec((1, DIFF_V_DIM))],
        out_specs=head,
        out_shape=jax.ShapeDtypeStruct((B, L, DIFF_WIDTH), BF16),
        scratch_shapes=[
            pltpu.VMEM((2 * tq, L), F32),
            pltpu.VMEM((2 * tq, L), BF16),
            pltpu.VMEM((L, DIFF_V_DIM + LANES), BF16),
        ],
        compiler_params=_params("arbitrary", "arbitrary"),
        name="diff_attention",
    )(diff_lambda, q, k, v, subln_w.reshape(1, DIFF_V_DIM))


def _pack_w_in(w):
    z, xbc, dt, q, k, v = jnp.split(w, (512, 1536, 1544, 2056, 2568), axis=-1)
    dt = jnp.pad(dt, ((0, 0), (0, DT_PAD - SSD_HEADS)))
    return jnp.concatenate([z, xbc, q, k, v, dt], axis=-1).astype(BF16)


def kernel(x, c, positions, w_ada, b_ada, w_ffn_in, w_ffn_out, w_in, conv_w, conv_b, dt_bias, a_log,
           d_skip, ssd_norm_w, diff_lambda, subln_w, w_out, ln_g, ln_b):
    B = x.shape[0]
    mod_all = _ada_mod(c, w_ada, b_ada).reshape(DEPTH, B, 3 * N_SUB, D_MODEL)
    cos, sin = _rope_tables(positions)
    for layer in range(DEPTH):
        mod = mod_all[layer]
        lam_init = 0.8 - 0.6 * float(np.exp(-0.3 * layer))
        g = ln_g[layer].reshape(N_SUB, 1, D_MODEL)
        b = ln_b[layer].reshape(N_SUB, 1, D_MODEL)
        x = _ffn(x, mod, w_ffn_in[layer, 0].astype(BF16), w_ffn_out[layer, 0].astype(BF16), g[0], b[0], 0)
        z, xbc, q, k, v, dt_raw = _inproj(x, mod, _pack_w_in(w_in[layer]), cos, sin)
        y = _ssd(xbc, dt_raw, z, conv_w[layer], conv_b[layer], dt_bias[layer], a_log[layer],
                 d_skip[layer], ssd_norm_w[layer])
        o = _attention(q, k, v, diff_lambda[layer], subln_w[layer], lam_init)
        x = _mixer_out_ffn(x, y, o, mod, w_out[layer].astype(BF16), g[1], b[1],
                           w_ffn_in[layer, 1].astype(BF16), w_ffn_out[layer, 1].astype(BF16), g[2], b[2])
    return x
```

```python
import functools

import numpy as np
import jax
import jax.numpy as jnp
from jax import lax
from jax.experimental import pallas as pl
from jax.experimental.pallas import tpu as pltpu

F32 = jnp.float32
BF16 = jnp.bfloat16

D_MODEL = 1024
DEPTH = 4
CHUNK = 64
SSD_HEADS = 8
SSD_HEAD_DIM = 64
SSD_WIDTH = 512
SSD_GROUPS = 2
SSD_STATE = 128
SSD_CONV = 4
DIFF_HEADS = 4
DIFF_HEAD_DIM = 64
DIFF_V_DIM = 128
DIFF_WIDTH = 512
ROT_DIM = 16
ROPE_THETA = 500000.0
CONV_CH = 1024
QK_WIDTH = 512
D_FF = 2816
N_SUB = 3
LN_EPS = 1e-5
DEEPNORM_ALPHA = (2 * DEPTH) ** 0.25
FFN_RES_WEIGHT = 0.5
LOG2_E = 1.4426950408889634

LANES = 128
SUBLANES = 8
VMEM_LIMIT = 56 * 1024 * 1024

TOKEN_TILE = 512
FFN_TOKEN_TILE = 1024
FFN_SUBTILE = 512
FFN_PIECE = 64
FFN_CHUNKS = 11
ATTN_Q_TILE = 256
ATTN_KEY_BLOCK = 512
ATTN_ROW_GROUP = 64
SSD_TILE = 256
SSD_Q = 128
DT_PAD = LANES
PROJ_COLS = SSD_WIDTH + CONV_CH + 2 * QK_WIDTH + DIFF_WIDTH + DT_PAD


def _params(*sem):
    return pltpu.CompilerParams(dimension_semantics=sem, vmem_limit_bytes=VMEM_LIMIT)


def _const_spec(shape, lead=()):
    nd = len(shape)
    return pl.BlockSpec((None,) * len(lead) + tuple(shape), lambda *_: tuple(lead) + (0,) * nd,
                        pipeline_mode=pl.Buffered(1))


def _head_major_spec(tm):
    return pl.BlockSpec((None, DIFF_HEADS, tm, LANES), lambda b, i: (b, 0, i, 0))


def _standardize(x):
    mu = jnp.mean(x, axis=-1, keepdims=True)
    xc = x - mu
    var = jnp.mean(xc * xc, axis=-1, keepdims=True)
    return xc * lax.rsqrt(var + LN_EPS)


def _silu(x):
    return x * jax.nn.sigmoid(x)


def _dot(a, b):
    return jnp.dot(a, b, preferred_element_type=F32)


def _dot_nt(a, b):
    return lax.dot_general(a, b, (((1,), (1,)), ((), ())), preferred_element_type=F32)


def _dot_tn(a, b):
    return lax.dot_general(a, b, (((0,), (0,)), ((), ())), preferred_element_type=F32)


def _split3(x):
    hi = x.astype(BF16)
    r1 = x - hi.astype(F32)
    mid = r1.astype(BF16)
    lo = (r1 - mid.astype(F32)).astype(BF16)
    return hi, mid, lo


def _select_exact(x, sel3):
    return _dot(jnp.concatenate(_split3(x), axis=1), sel3)


def _prefix_sum_exact(tri3, x):
    return _dot(tri3, jnp.concatenate(_split3(x), axis=0))


def _ada_kernel(c_ref, w_ref, b_ref, o_ref):
    sc = _silu(c_ref[...]).astype(BF16)
    o_ref[...] = _dot(sc, w_ref[...].astype(BF16)) + b_ref[...]


def _ada_mod(c, w_ada, b_ada):
    B = c.shape[0]
    ncol = w_ada.shape[-1]
    tn = 1024
    return pl.pallas_call(
        _ada_kernel,
        grid=(DEPTH, ncol // tn),
        in_specs=[
            pl.BlockSpec((B, D_MODEL), lambda l, j: (0, 0)),
            pl.BlockSpec((None, D_MODEL, tn), lambda l, j: (l, 0, j)),
            pl.BlockSpec((None, 1, tn), lambda l, j: (l, 0, j)),
        ],
        out_specs=pl.BlockSpec((None, B, tn), lambda l, j: (l, 0, j)),
        out_shape=jax.ShapeDtypeStruct((DEPTH, B, ncol), F32),
        compiler_params=_params("arbitrary", "arbitrary"),
        name="ada_mod",
    )(c, w_ada, b_ada.reshape(DEPTH, 1, ncol))


def _rope_kernel(pos_ref, inv_ref, cos_ref, sin_ref):
    ang = pos_ref[...].astype(F32) * inv_ref[...]
    d = lax.broadcasted_iota(jnp.int32, ang.shape, 1) & (DIFF_HEAD_DIM - 1)
    c = jnp.cos(ang)
    s = jnp.sin(ang)
    cos_ref[...] = jnp.where(d < ROT_DIM, c, 1.0)
    sin_ref[...] = jnp.where(d < ROT_DIM // 2, -s, jnp.where(d < ROT_DIM, s, 0.0))


def _rope_tables(positions):
    B, L = positions.shape
    half = ROT_DIM // 2
    inv = np.float32(ROPE_THETA) ** (-np.arange(half, dtype=np.float32) * np.float32(2.0) / np.float32(ROT_DIM))
    inv_lane = jnp.asarray(np.tile(inv.astype(np.float32), LANES // half)[None, :])
    tm = min(L, 1024)
    spec = pl.BlockSpec((None, tm, LANES), lambda b, i: (b, i, 0))
    return pl.pallas_call(
        _rope_kernel,
        grid=(B, L // tm),
        in_specs=[pl.BlockSpec((None, tm, 1), lambda b, i: (b, i, 0)),
                  pl.BlockSpec((1, LANES), lambda b, i: (0, 0))],
        out_specs=[spec, spec],
        out_shape=[jax.ShapeDtypeStruct((B, L, LANES), F32)] * 2,
        compiler_params=_params("arbitrary", "arbitrary"),
        name="rope_tables",
    )(positions.reshape(B, L, 1), inv_lane)


def _zero_after(x, width):
    bits = pltpu.bitcast(x, jnp.int32)
    acc = bits[0:SUBLANES, :]
    for r in range(SUBLANES, x.shape[0], SUBLANES):
        acc = acc | bits[r:r + SUBLANES, :]
    lane = acc[:, 0:LANES]
    for c in range(LANES, x.shape[1], LANES):
        lane = lane | acc[:, c:c + LANES]
    sixteen = jnp.full(lane.shape, 16, jnp.int32)
    zero = lax.shift_right_logical(lax.shift_right_logical(lane, sixteen), sixteen).astype(F32)
    return jnp.concatenate([zero] * (width // LANES), axis=1)


def _ffn_block(n_rows, sub, stream_piece, next_tile_matmul, mod_ref, win_ref, wout_ref, lng_ref, lnb_ref,
               o_ref, u_ref, f_ref):
    shift = mod_ref[3 * sub:3 * sub + 1, :]
    scale1 = 1.0 + mod_ref[3 * sub + 1:3 * sub + 2, :]
    res_gain = FFN_RES_WEIGHT * (1.0 + mod_ref[3 * sub + 2:3 * sub + 3, :])
    tiles = n_rows // FFN_SUBTILE
    pieces = FFN_SUBTILE // FFN_PIECE
    fc = D_FF // FFN_CHUNKS
    assert pieces <= FFN_CHUNKS

    def tile_rows(t):
        return slice(t * FFN_SUBTILE, (t + 1) * FFN_SUBTILE)

    def modulate_piece(t, p):
        rows = slice(t * FFN_SUBTILE + p * FFN_PIECE, t * FFN_SUBTILE + (p + 1) * FFN_PIECE)
        s = stream_piece(rows)
        o_ref[rows, :] = s
        u = _standardize(s) * scale1 + shift
        u_ref[t % 2, p * FFN_PIECE:(p + 1) * FFN_PIECE, :] = u.astype(BF16)
        return u

    def norm_piece(t, p):
        rows = slice(t * FFN_SUBTILE + p * FFN_PIECE, t * FFN_SUBTILE + (p + 1) * FFN_PIECE)
        y = DEEPNORM_ALPHA * o_ref[rows, :] + res_gain * f_ref[p * FFN_PIECE:(p + 1) * FFN_PIECE, :]
        o_ref[rows, :] = _standardize(y) * lng_ref[...] + lnb_ref[...]

    if next_tile_matmul is not None:
        next_tile_matmul(tile_rows(0))
    for p in range(pieces):
        modulate_piece(0, p)
    for t in range(tiles):
        if next_tile_matmul is not None and t + 1 < tiles:
            next_tile_matmul(tile_rows(t + 1))
        f = None
        pending = None
        for k in range(FFN_CHUNKS):
            u = u_ref[t % 2]
            g = _dot(u, win_ref[:, k * fc:(k + 1) * fc])
            up = _dot(u, win_ref[:, D_FF + k * fc:D_FF + (k + 1) * fc])
            if pending is not None:
                up = jnp.concatenate([up[0:SUBLANES] + _zero_after(pending, fc), up[SUBLANES:]], axis=0)
                pending = None
            h = (_silu(g) * up).astype(BF16)
            fk = _dot(h, wout_ref[k * fc:(k + 1) * fc, :])
            f = fk if f is None else f + fk
            if k < pieces:
                if t + 1 < tiles:
                    pending = modulate_piece(t + 1, k)
                if t >= 1:
                    norm_piece(t - 1, k)
        f_ref[...] = f
    for p in range(pieces):
        norm_piece(tiles - 1, p)


def _ffn_kernel(x_ref, mod_ref, win_ref, wout_ref, lng_ref, lnb_ref, o_ref, u_ref, f_ref, *, sub):
    _ffn_block(x_ref.shape[0], sub, lambda rows: x_ref[rows, :], None,
               mod_ref, win_ref, wout_ref, lng_ref, lnb_ref, o_ref, u_ref, f_ref)


def _mixer_out_ffn_kernel(x_ref, y_ref, a_ref, mod_ref, wmix_ref, lng1_ref, lnb1_ref,
                          win_ref, wout_ref, lng_ref, lnb_ref, o_ref, u_ref, f_ref, m_ref):
    gain = 1.0 + mod_ref[5:6, :]

    def mixer_matmul(rows):
        a = jnp.concatenate([a_ref[h, rows, :] for h in range(DIFF_HEADS)], axis=1)
        m_ref[...] = _dot(y_ref[rows, :], wmix_ref[0:SSD_WIDTH, :]) + _dot(a, wmix_ref[SSD_WIDTH:, :])

    def stream_piece(rows):
        local = slice(rows.start % FFN_SUBTILE, rows.start % FFN_SUBTILE + FFN_PIECE)
        t = DEEPNORM_ALPHA * x_ref[rows, :] + gain * m_ref[local, :]
        return _standardize(t) * lng1_ref[...] + lnb1_ref[...]

    _ffn_block(x_ref.shape[0], 2, stream_piece, mixer_matmul,
               mod_ref, win_ref, wout_ref, lng_ref, lnb_ref, o_ref, u_ref, f_ref)


def _ffn_scratch():
    return [pltpu.VMEM((2, FFN_SUBTILE, D_MODEL), BF16), pltpu.VMEM((FFN_SUBTILE, D_MODEL), F32)]


def _ffn_specs(B, L, layer, which):
    tm = min(L, FFN_TOKEN_TILE)

    def tok(width):
        return pl.BlockSpec((None, tm, width), lambda b, i: (b, i, 0))

    mod_spec = pl.BlockSpec((None, 3 * N_SUB, D_MODEL), lambda b, i: (b, 0, 0))
    weights = [_const_spec((D_MODEL, 2 * D_FF), (layer, which)), _const_spec((D_FF, D_MODEL), (layer, which)),
               _const_spec((1, D_MODEL)), _const_spec((1, D_MODEL))]
    return (B, L // tm), tm, tok, mod_spec, weights


def _ffn(x, mod, w_in, w_out, ln_g, ln_b, sub, layer):
    B, L, _ = x.shape
    grid, _, tok, mod_spec, weights = _ffn_specs(B, L, layer, 0)
    return pl.pallas_call(
        functools.partial(_ffn_kernel, sub=sub),
        grid=grid,
        in_specs=[tok(D_MODEL), mod_spec] + weights,
        out_specs=tok(D_MODEL),
        out_shape=jax.ShapeDtypeStruct(x.shape, F32),
        scratch_shapes=_ffn_scratch(),
        compiler_params=_params("arbitrary", "arbitrary"),
        name=f"ffn{sub}",
    )(x, mod, w_in, w_out, ln_g, ln_b)


def _mixer_out_ffn(x, y, a, mod, w_mix, ln_g1, ln_b1, w_in, w_out, ln_g, ln_b, layer):
    B, L, _ = x.shape
    grid, grid_tm, tok, mod_spec, weights = _ffn_specs(B, L, layer, 1)
    return pl.pallas_call(
        _mixer_out_ffn_kernel,
        grid=grid,
        in_specs=[tok(D_MODEL), tok(SSD_WIDTH), _head_major_spec(grid_tm), mod_spec,
                  _const_spec((SSD_WIDTH + DIFF_WIDTH, D_MODEL), (layer,)),
                  _const_spec((1, D_MODEL)), _const_spec((1, D_MODEL))]
        + weights,
        out_specs=tok(D_MODEL),
        out_shape=jax.ShapeDtypeStruct(x.shape, F32),
        scratch_shapes=_ffn_scratch() + [pltpu.VMEM((FFN_SUBTILE, D_MODEL), F32)],
        compiler_params=_params("arbitrary", "arbitrary"),
        name="mixer_out_ffn2",
    )(x, y, a, mod, w_mix, ln_g1, ln_b1, w_in, w_out, ln_g, ln_b)


_Z0 = 0
_XBC0 = SSD_WIDTH
_Q0 = _XBC0 + CONV_CH
_K0 = _Q0 + QK_WIDTH
_V0 = _K0 + QK_WIDTH
_DT0 = _V0 + DIFF_WIDTH


def _inproj_kernel(x_ref, mod_ref, w_ref, cos_ref, sin_ref,
                   z_ref, xbc_ref, q_ref, k_ref, v_ref, dt_ref):
    x = x_ref[...]
    shift = mod_ref[3:4, :]
    scale = mod_ref[4:5, :]
    u = (_standardize(x) * (1.0 + scale) + shift).astype(BF16)
    z_ref[...] = _dot(u, w_ref[:, _Z0:_XBC0])
    xbc_ref[...] = _dot(u, w_ref[:, _XBC0:_Q0])
    vv = _dot(u, w_ref[:, _V0:_DT0]).astype(BF16)
    for h in range(DIFF_HEADS):
        v_ref[h, :, :] = vv[:, h * LANES:(h + 1) * LANES]
    dt_ref[...] = _dot(u, w_ref[:, _DT0:_DT0 + DT_PAD])
    cos = cos_ref[...]
    sin = sin_ref[...]
    d = lax.broadcasted_iota(jnp.int32, cos.shape, 1) & (DIFF_HEAD_DIM - 1)
    first_half = d < ROT_DIM // 2
    half = ROT_DIM // 2
    q_scale = DIFF_HEAD_DIM ** -0.5 * LOG2_E
    for col0, ref, mult in ((_Q0, q_ref, q_scale), (_K0, k_ref, None)):
        for pair in range(DIFF_HEADS // 2):
            lo2 = pair * 2 * LANES
            t2 = _dot(u, w_ref[:, col0 + lo2:col0 + lo2 + 2 * LANES])
            for j in range(2):
                t = t2[:, j * LANES:(j + 1) * LANES]
                partner = jnp.where(first_half, pltpu.roll(t, LANES - half, axis=1), pltpu.roll(t, half, axis=1))
                r = t * cos + partner * sin
                if mult is not None:
                    r = r * mult
                ref[2 * pair + j, :, :] = r.astype(BF16)


def _inproj(x, mod, w, cos, sin, layer):
    B, L, _ = x.shape
    tm = min(L, TOKEN_TILE)

    def tok(width):
        return pl.BlockSpec((None, tm, width), lambda b, i: (b, i, 0))

    return pl.pallas_call(
        _inproj_kernel,
        grid=(B, L // tm),
        in_specs=[
            tok(D_MODEL),
            pl.BlockSpec((None, 3 * N_SUB, D_MODEL), lambda b, i: (b, 0, 0)),
            _const_spec((D_MODEL, PROJ_COLS), (layer,)),
            tok(LANES), tok(LANES),
        ],
        out_specs=[tok(SSD_WIDTH), tok(CONV_CH), _head_major_spec(tm), _head_major_spec(tm), _head_major_spec(tm),
                   tok(DT_PAD)],
        out_shape=[
            jax.ShapeDtypeStruct((B, L, SSD_WIDTH), F32),
            jax.ShapeDtypeStruct((B, L, CONV_CH), F32),
            jax.ShapeDtypeStruct((B, DIFF_HEADS, L, LANES), BF16),
            jax.ShapeDtypeStruct((B, DIFF_HEADS, L, LANES), BF16),
            jax.ShapeDtypeStruct((B, DIFF_HEADS, L, LANES), BF16),
            jax.ShapeDtypeStruct((B, L, DT_PAD), F32),
        ],
        compiler_params=_params("arbitrary", "arbitrary"),
        name="mixer_inproj",
    )(x, mod, w, cos, sin)


def _ssd_kernel(xbc_ref, dt_ref, z_ref, convw_ref, convb_ref, dtb_ref, alog_ref, dskip_ref, normw_ref,
                sel3_ref, tri3_ref, y_ref, xpad_ref, state_ref, *, tl):
    halo = SUBLANES
    hg = SSD_HEADS // SSD_GROUPS
    gw = hg * SSD_HEAD_DIM

    @pl.when(pl.program_id(1) == 0)
    def _():
        xpad_ref[0:halo, :] = jnp.zeros((halo, CONV_CH), F32)
        state_ref[...] = jnp.zeros(state_ref.shape, F32)

    xpad_ref[halo:halo + tl, :] = xbc_ref[...]
    xpad = xpad_ref[...]
    conv = convb_ref[...] + convw_ref[SSD_CONV - 1:SSD_CONV, :] * xpad[halo:halo + tl, :]
    for back in range(1, SSD_CONV):
        k = SSD_CONV - 1 - back
        conv = conv + convw_ref[k:k + 1, :] * pltpu.roll(xpad, back, axis=0)[halo:halo + tl, :]
    xpad_ref[0:halo, :] = xpad_ref[tl:tl + halo, :]
    act = _silu(conv)
    xs = act[:, :SSD_WIDTH]
    bm = act[:, SSD_WIDTH:SSD_WIDTH + SSD_GROUPS * SSD_STATE].astype(BF16)
    cm = act[:, SSD_WIDTH + SSD_GROUPS * SSD_STATE:].astype(BF16)

    lane = lax.broadcasted_iota(jnp.int32, (1, LANES), 1)
    v = dt_ref[...] + dtb_ref[...]
    dt = jnp.maximum(v, 0.0) + jnp.log1p(jnp.exp(-jnp.abs(v)))
    a = jnp.where(lane < SSD_HEADS, dt * (-jnp.exp(alog_ref[...])), 0.0)

    li = lax.broadcasted_iota(jnp.int32, (SSD_Q, SSD_Q), 0)
    si = lax.broadcasted_iota(jnp.int32, (SSD_Q, SSD_Q), 1)
    causal = li >= si
    sel3 = sel3_ref[...]
    tri3 = tri3_ref[...]

    xdt = xs * _select_exact(dt, sel3_ref[:, :SSD_WIDTH])
    lo_half = lax.broadcasted_iota(jnp.int32, (1, LANES), 1) < SSD_HEAD_DIM

    y_blocks = []
    for c in range(tl // SSD_Q):
        r0 = c * SSD_Q
        acs = _prefix_sum_exact(tri3, a[r0:r0 + SSD_Q, :])
        acs_t = acs.T
        acs_wide = _select_exact(acs, sel3)
        acs_head = acs_wide[:, :SSD_WIDTH]
        acs_tile = acs_wide[:, SSD_WIDTH:]
        last_head = acs_head[SSD_Q - 1:SSD_Q, :]
        grow = jnp.exp(acs_head)
        to_end = jnp.exp(last_head - acs_head)
        xdt_c = xdt[r0:r0 + SSD_Q, :]
        y_groups = []
        for g in range(SSD_GROUPS):
            bg = bm[r0:r0 + SSD_Q, g * SSD_STATE:(g + 1) * SSD_STATE]
            cg = cm[r0:r0 + SSD_Q, g * SSD_STATE:(g + 1) * SSD_STATE]
            cb = _dot_nt(cg, bg)
            st = state_ref[g]
            y_off = _dot_nt(cg, st.astype(BF16))
            y_pairs = []
            for p in range(hg // 2):
                ms, rhs = [], []
                xp = xdt_c[:, g * gw + p * LANES:g * gw + (p + 1) * LANES].astype(BF16)
                for j in range(2):
                    h = g * hg + 2 * p + j
                    seg = acs_tile[:, h * LANES:(h + 1) * LANES] - acs_t[h:h + 1, :]
                    decay = jnp.exp(jnp.where(causal, seg, -jnp.inf))
                    ms.append((cb * decay).astype(BF16))
                    keep = lo_half if j == 0 else jnp.logical_not(lo_half)
                    rhs.append(jnp.where(keep, xp, jnp.zeros_like(xp)))
                y_pairs.append(_dot(jnp.concatenate(ms, axis=1), jnp.concatenate(rhs, axis=0)))
            y_diag = jnp.concatenate(y_pairs, axis=1)
            y_groups.append(y_diag + y_off * grow[:, g * gw:(g + 1) * gw])
            wgt = (xdt_c[:, g * gw:(g + 1) * gw] * to_end[:, g * gw:(g + 1) * gw]).astype(BF16)
            d_state = _dot_tn(wgt, bg)
            scale_rows = []
            for j in range(hg):
                h = g * hg + j
                scale_rows.append(jnp.broadcast_to(jnp.exp(acs_t[h:h + 1, SSD_Q - 1:SSD_Q]), (SSD_HEAD_DIM, SSD_STATE)))
            state_ref[g] = st * jnp.concatenate(scale_rows, axis=0) + d_state
        y_blocks.append(jnp.concatenate(y_groups, axis=1))
    y = jnp.concatenate(y_blocks, axis=0) if len(y_blocks) > 1 else y_blocks[0]
    y = y + xs * dskip_ref[...]
    y = y * _silu(z_ref[...])
    outs = []
    for g in range(SSD_GROUPS):
        yg = y[:, g * gw:(g + 1) * gw]
        ms = jnp.mean(yg * yg, axis=-1, keepdims=True)
        outs.append(yg * lax.rsqrt(ms + LN_EPS) * normw_ref[:, g * gw:(g + 1) * gw])
    y_ref[...] = jnp.concatenate(outs, axis=1).astype(BF16)


def _ssd_constants():
    wide = SSD_WIDTH + SSD_HEADS * LANES
    row = np.arange(3 * LANES)[:, None] % LANES
    col = np.arange(wide)[None, :]
    col_head = np.where(col < SSD_WIDTH, col // SSD_HEAD_DIM, (col - SSD_WIDTH) // LANES)
    sel3 = (row == col_head).astype(np.float32)
    li = np.arange(SSD_Q)[:, None]
    si = np.arange(3 * SSD_Q)[None, :] % SSD_Q
    tri3 = (li >= si).astype(np.float32)
    return jnp.asarray(sel3, BF16), jnp.asarray(tri3, BF16)


def _ssd(xbc, dt_raw, z, conv_w, conv_b, dt_bias, a_log, d_skip, norm_w):
    B, L, _ = xbc.shape
    tl = min(L, SSD_TILE)
    sel3, tri3 = _ssd_constants()

    def tok(width):
        return pl.BlockSpec((None, tl, width), lambda b, i: (b, i, 0))

    def pad_heads(p):
        return jnp.zeros((1, DT_PAD), F32).at[0, :SSD_HEADS].set(p)

    return pl.pallas_call(
        functools.partial(_ssd_kernel, tl=tl),
        grid=(B, L // tl),
        in_specs=[
            tok(CONV_CH), tok(DT_PAD), tok(SSD_WIDTH),
            _const_spec((SSD_CONV, CONV_CH)), _const_spec((1, CONV_CH)),
            _const_spec((1, DT_PAD)), _const_spec((1, DT_PAD)),
            _const_spec((1, SSD_WIDTH)), _const_spec((1, SSD_WIDTH)),
            _const_spec(sel3.shape), _const_spec(tri3.shape),
        ],
        out_specs=tok(SSD_WIDTH),
        out_shape=jax.ShapeDtypeStruct((B, L, SSD_WIDTH), BF16),
        scratch_shapes=[
            pltpu.VMEM((tl + SUBLANES, CONV_CH), F32),
            pltpu.VMEM((SSD_GROUPS, (SSD_HEADS // SSD_GROUPS) * SSD_HEAD_DIM, SSD_STATE), F32),
        ],
        compiler_params=_params("arbitrary", "arbitrary"),
        name="ssd_scan",
    )(xbc, dt_raw, z, conv_w, conv_b.reshape(1, CONV_CH), pad_heads(dt_bias), pad_heads(a_log),
      jnp.repeat(d_skip, SSD_HEAD_DIM).reshape(1, SSD_WIDTH), norm_w.reshape(1, SSD_WIDTH), sel3, tri3)


def _attn_kernel(lam_ref, q_ref, k_ref, v_ref, subw_ref, o_ref, s_ref, e_ref, vx_ref, *, seq, tq, lam_init):
    rows = 2 * tq
    vx_ref[:, 0:DIFF_V_DIM] = v_ref[...]
    vx_ref[:, DIFF_V_DIM:] = jnp.ones((seq, LANES), BF16)
    lp = lam_ref[...]
    lam = (jnp.exp(jnp.sum(lp[0:1] * lp[1:2], axis=-1, keepdims=True))
           - jnp.exp(jnp.sum(lp[2:3] * lp[3:4], axis=-1, keepdims=True)) + lam_init)
    lo_half = lax.broadcasted_iota(jnp.int32, (1, LANES), 1) < DIFF_HEAD_DIM
    qchunk = (lax.broadcasted_iota(jnp.int32, (rows, tq), 0) & (tq - 1)) >> 6
    kchunk = lax.broadcasted_iota(jnp.int32, (rows, tq), 1) >> 6
    allowed = kchunk <= qchunk
    for qi in reversed(range(seq // tq)):
        r0 = qi * tq
        ext = r0 + tq
        q = q_ref[r0:r0 + tq, :]
        zero = jnp.zeros_like(q)
        q2 = jnp.concatenate([jnp.where(lo_half, q, zero), jnp.where(lo_half, zero, q)], axis=0)
        blocks = [(c0, min(ATTN_KEY_BLOCK, r0 - c0)) for c0 in range(0, r0, ATTN_KEY_BLOCK)] + [(r0, tq)]
        for c0, cw in blocks:
            sc = _dot_nt(q2, k_ref[c0:c0 + cw, :])
            if c0 == r0:
                sc = jnp.where(allowed, sc, -jnp.inf)
            s_ref[:, c0:c0 + cw] = sc
        for rg in range(rows // ATTN_ROW_GROUP):
            rsl = slice(rg * ATTN_ROW_GROUP, (rg + 1) * ATTN_ROW_GROUP)
            mx = s_ref[rsl, 0:LANES]
            for j in range(1, ext // LANES):
                mx = jnp.maximum(mx, s_ref[rsl, j * LANES:(j + 1) * LANES])
            mb = jnp.broadcast_to(jnp.max(mx, axis=-1, keepdims=True), (ATTN_ROW_GROUP, LANES))
            for j in range(ext // LANES):
                csl = slice(j * LANES, (j + 1) * LANES)
                e_ref[rsl, csl] = jnp.exp2(s_ref[rsl, csl] - mb).astype(BF16)
        acc = _dot(e_ref[:, 0:ext], vx_ref[0:ext, :])
        inv = 1.0 / acc[:, DIFF_V_DIM:]
        acc = acc[:, :DIFF_V_DIM]
        o = acc[:tq] * inv[:tq] - acc[tq:] * (lam * inv[tq:])
        o = o * lax.rsqrt(jnp.mean(o * o, axis=-1, keepdims=True) + LN_EPS) * subw_ref[...] * (1.0 - lam_init)
        o_ref[r0:r0 + tq, :] = o.astype(BF16)


def _attention(q, k, v, diff_lambda, subln_w, lam_init):
    B, _, L, _ = q.shape
    tq = min(L, ATTN_Q_TILE)
    head = pl.BlockSpec((None, None, L, LANES), lambda b, h: (b, h, 0, 0))
    return pl.pallas_call(
        functools.partial(_attn_kernel, seq=L, tq=tq, lam_init=lam_init),
        grid=(B, DIFF_HEADS),
        in_specs=[_const_spec((4, DIFF_HEAD_DIM)), head, head, head, _const_spec((1, DIFF_V_DIM))],
        out_specs=head,
        out_shape=jax.ShapeDtypeStruct((B, DIFF_HEADS, L, LANES), BF16),
        scratch_shapes=[
            pltpu.VMEM((2 * tq, L), F32),
            pltpu.VMEM((2 * tq, L), BF16),
            pltpu.VMEM((L, DIFF_V_DIM + LANES), BF16),
        ],
        compiler_params=_params("arbitrary", "arbitrary"),
        name="diff_attention",
    )(diff_lambda, q, k, v, subln_w.reshape(1, DIFF_V_DIM))


def _pack_w_in(w):
    z, xbc, dt, q, k, v = jnp.split(w, (512, 1536, 1544, 2056, 2568), axis=-1)
    dt = jnp.pad(dt, ((0, 0),) * (w.ndim - 1) + ((0, DT_PAD - SSD_HEADS),))
    return jnp.concatenate([z, xbc, q, k, v, dt], axis=-1).astype(BF16)


def kernel(x, c, positions, w_ada, b_ada, w_ffn_in, w_ffn_out, w_in, conv_w, conv_b, dt_bias, a_log,
           d_skip, ssd_norm_w, diff_lambda, subln_w, w_out, ln_g, ln_b):
    B = x.shape[0]
    mod_all = _ada_mod(c, w_ada, b_ada).reshape(DEPTH, B, 3 * N_SUB, D_MODEL)
    cos, sin = _rope_tables(positions)
    w_ffn_in, w_ffn_out, w_out = (w.astype(BF16) for w in (w_ffn_in, w_ffn_out, w_out))
    w_proj = _pack_w_in(w_in)
    for layer in range(DEPTH):
        mod = mod_all[layer]
        lam_init = 0.8 - 0.6 * float(np.exp(-0.3 * layer))
        g = ln_g[layer].reshape(N_SUB, 1, D_MODEL)
        b = ln_b[layer].reshape(N_SUB, 1, D_MODEL)
        x = _ffn(x, mod, w_ffn_in, w_ffn_out, g[0], b[0], 0, layer)
        z, xbc, q, k, v, dt_raw = _inproj(x, mod, w_proj, cos, sin, layer)
        y = _ssd(xbc, dt_raw, z, conv_w[layer], conv_b[layer], dt_bias[layer], a_log[layer],
                 d_skip[layer], ssd_norm_w[layer])
        o = _attention(q, k, v, diff_lambda[layer], subln_w[layer], lam_init)
        x = _mixer_out_ffn(x, y, o, mod, w_out, g[1], b[1], w_ffn_in, w_ffn_out, g[2], b[2], layer)
    return x
```

```python
import functools

import numpy as np
import jax
import jax.numpy as jnp
from jax import lax
from jax.experimental import pallas as pl
from jax.experimental.pallas import tpu as pltpu

F32 = jnp.float32
BF16 = jnp.bfloat16

D_MODEL = 1024
DEPTH = 4
CHUNK = 64
SSD_HEADS = 8
SSD_HEAD_DIM = 64
SSD_WIDTH = 512
SSD_GROUPS = 2
SSD_STATE = 128
SSD_CONV = 4
DIFF_HEADS = 4
DIFF_HEAD_DIM = 64
DIFF_V_DIM = 128
DIFF_WIDTH = 512
ROT_DIM = 16
ROPE_THETA = 500000.0
CONV_CH = 1024
QK_WIDTH = 512
D_FF = 2816
N_SUB = 3
LN_EPS = 1e-5
DEEPNORM_ALPHA = (2 * DEPTH) ** 0.25
FFN_RES_WEIGHT = 0.5
LOG2_E = 1.4426950408889634

LANES = 128
SUBLANES = 8
VMEM_LIMIT = 56 * 1024 * 1024

TOKEN_TILE = 512
FFN_TOKEN_TILE = 1024
FFN_SUBTILE = 512
FFN_PIECE = 64
FFN_CHUNKS = 11
ATTN_Q_TILE = 256
ATTN_KEY_BLOCK = 512
ATTN_ROW_GROUP = 64
SSD_TILE = 512
SSD_Q = 128
DT_PAD = LANES
PROJ_COLS = SSD_WIDTH + CONV_CH + 2 * QK_WIDTH + DIFF_WIDTH + DT_PAD


def _params(*sem):
    return pltpu.CompilerParams(dimension_semantics=sem, vmem_limit_bytes=VMEM_LIMIT)


def _const_spec(shape, lead=()):
    nd = len(shape)
    return pl.BlockSpec((None,) * len(lead) + tuple(shape), lambda *_: tuple(lead) + (0,) * nd,
                        pipeline_mode=pl.Buffered(1))


def _head_major_spec(tm):
    return pl.BlockSpec((None, DIFF_HEADS, tm, LANES), lambda b, i: (b, 0, i, 0))


def _standardize(x):
    mu = jnp.mean(x, axis=-1, keepdims=True)
    xc = x - mu
    var = jnp.mean(xc * xc, axis=-1, keepdims=True)
    return xc * lax.rsqrt(var + LN_EPS)


def _silu(x):
    h = 0.5 * x
    return h + h * jnp.tanh(h)


def _dot(a, b):
    return jnp.dot(a, b, preferred_element_type=F32)


def _dot_nt(a, b):
    return lax.dot_general(a, b, (((1,), (1,)), ((), ())), preferred_element_type=F32)


def _dot_tn(a, b):
    return lax.dot_general(a, b, (((0,), (0,)), ((), ())), preferred_element_type=F32)


def _split3(x):
    hi = x.astype(BF16)
    r1 = x - hi.astype(F32)
    mid = r1.astype(BF16)
    lo = (r1 - mid.astype(F32)).astype(BF16)
    return hi, mid, lo


def _select_exact(x, sel3):
    return _dot(jnp.concatenate(_split3(x), axis=1), sel3)


def _prefix_sum_exact(tri3, x):
    return _dot(tri3, jnp.concatenate(_split3(x), axis=0))


def _ada_kernel(c_ref, w_ref, b_ref, o_ref):
    sc = _silu(c_ref[...]).astype(BF16)
    o_ref[...] = _dot(sc, w_ref[...].astype(BF16)) + b_ref[...]


def _ada_mod(c, w_ada, b_ada):
    B = c.shape[0]
    ncol = w_ada.shape[-1]
    tn = 1024
    return pl.pallas_call(
        _ada_kernel,
        grid=(DEPTH, ncol // tn),
        in_specs=[
            pl.BlockSpec((B, D_MODEL), lambda l, j: (0, 0)),
            pl.BlockSpec((None, D_MODEL, tn), lambda l, j: (l, 0, j)),
            pl.BlockSpec((None, 1, tn), lambda l, j: (l, 0, j)),
        ],
        out_specs=pl.BlockSpec((None, B, tn), lambda l, j: (l, 0, j)),
        out_shape=jax.ShapeDtypeStruct((DEPTH, B, ncol), F32),
        compiler_params=_params("arbitrary", "arbitrary"),
        name="ada_mod",
    )(c, w_ada, b_ada.reshape(DEPTH, 1, ncol))


def _rope_kernel(pos_ref, inv_ref, cos_ref, sin_ref):
    ang = pos_ref[...].astype(F32) * inv_ref[...]
    d = lax.broadcasted_iota(jnp.int32, ang.shape, 1) & (DIFF_HEAD_DIM - 1)
    c = jnp.cos(ang)
    s = jnp.sin(ang)
    cos_ref[...] = jnp.where(d < ROT_DIM, c, 1.0)
    sin_ref[...] = jnp.where(d < ROT_DIM // 2, -s, jnp.where(d < ROT_DIM, s, 0.0))


def _rope_tables(positions):
    B, L = positions.shape
    half = ROT_DIM // 2
    inv = np.float32(ROPE_THETA) ** (-np.arange(half, dtype=np.float32) * np.float32(2.0) / np.float32(ROT_DIM))
    inv_lane = jnp.asarray(np.tile(inv.astype(np.float32), LANES // half)[None, :])
    tm = min(L, 1024)
    spec = pl.BlockSpec((None, tm, LANES), lambda b, i: (b, i, 0))
    return pl.pallas_call(
        _rope_kernel,
        grid=(B, L // tm),
        in_specs=[pl.BlockSpec((None, tm, 1), lambda b, i: (b, i, 0)),
                  pl.BlockSpec((1, LANES), lambda b, i: (0, 0))],
        out_specs=[spec, spec],
        out_shape=[jax.ShapeDtypeStruct((B, L, LANES), F32)] * 2,
        compiler_params=_params("arbitrary", "arbitrary"),
        name="rope_tables",
    )(positions.reshape(B, L, 1), inv_lane)


def _zero_after(x, width):
    bits = pltpu.bitcast(x, jnp.int32)
    acc = bits[0:SUBLANES, :]
    for r in range(SUBLANES, x.shape[0], SUBLANES):
        acc = acc | bits[r:r + SUBLANES, :]
    lane = acc[:, 0:LANES]
    for c in range(LANES, x.shape[1], LANES):
        lane = lane | acc[:, c:c + LANES]
    sixteen = jnp.full(lane.shape, 16, jnp.int32)
    zero = lax.shift_right_logical(lax.shift_right_logical(lane, sixteen), sixteen).astype(F32)
    return jnp.concatenate([zero] * (width // LANES), axis=1)


def _ffn_block(n_rows, sub, stream_piece, next_tile_matmul, mod_ref, win_ref, wout_ref, lng_ref, lnb_ref,
               o_ref, u_ref, f_ref):
    shift = mod_ref[3 * sub:3 * sub + 1, :]
    scale1 = 1.0 + mod_ref[3 * sub + 1:3 * sub + 2, :]
    res_gain = FFN_RES_WEIGHT * (1.0 + mod_ref[3 * sub + 2:3 * sub + 3, :])
    tiles = n_rows // FFN_SUBTILE
    pieces = FFN_SUBTILE // FFN_PIECE
    fc = D_FF // FFN_CHUNKS
    assert pieces <= FFN_CHUNKS

    def tile_rows(t):
        return slice(t * FFN_SUBTILE, (t + 1) * FFN_SUBTILE)

    def modulate_piece(t, p):
        rows = slice(t * FFN_SUBTILE + p * FFN_PIECE, t * FFN_SUBTILE + (p + 1) * FFN_PIECE)
        s = stream_piece(rows)
        o_ref[rows, :] = s
        u = _standardize(s) * scale1 + shift
        u_ref[t % 2, p * FFN_PIECE:(p + 1) * FFN_PIECE, :] = u.astype(BF16)
        return u

    def norm_piece(t, p):
        rows = slice(t * FFN_SUBTILE + p * FFN_PIECE, t * FFN_SUBTILE + (p + 1) * FFN_PIECE)
        y = DEEPNORM_ALPHA * o_ref[rows, :] + res_gain * f_ref[p * FFN_PIECE:(p + 1) * FFN_PIECE, :]
        o_ref[rows, :] = _standardize(y) * lng_ref[...] + lnb_ref[...]

    if next_tile_matmul is not None:
        next_tile_matmul(tile_rows(0))
    for p in range(pieces):
        modulate_piece(0, p)
    for t in range(tiles):
        if next_tile_matmul is not None and t + 1 < tiles:
            next_tile_matmul(tile_rows(t + 1))
        f = None
        pending = None
        for k in range(FFN_CHUNKS):
            u = u_ref[t % 2]
            g = _dot(u, win_ref[:, k * fc:(k + 1) * fc])
            up = _dot(u, win_ref[:, D_FF + k * fc:D_FF + (k + 1) * fc])
            if pending is not None:
                up = jnp.concatenate([up[0:SUBLANES] + _zero_after(pending, fc), up[SUBLANES:]], axis=0)
                pending = None
            h = (_silu(g) * up).astype(BF16)
            fk = _dot(h, wout_ref[k * fc:(k + 1) * fc, :])
            f = fk if f is None else f + fk
            if k < pieces:
                if t + 1 < tiles:
                    pending = modulate_piece(t + 1, k)
                if t >= 1:
                    norm_piece(t - 1, k)
        f_ref[...] = f
    for p in range(pieces):
        norm_piece(tiles - 1, p)


def _ffn_kernel(x_ref, mod_ref, win_ref, wout_ref, lng_ref, lnb_ref, o_ref, u_ref, f_ref, *, sub):
    _ffn_block(x_ref.shape[0], sub, lambda rows: x_ref[rows, :], None,
               mod_ref, win_ref, wout_ref, lng_ref, lnb_ref, o_ref, u_ref, f_ref)


def _mixer_out_ffn_kernel(x_ref, y_ref, a_ref, mod_ref, wmix_ref, lng1_ref, lnb1_ref,
                          win_ref, wout_ref, lng_ref, lnb_ref, o_ref, u_ref, f_ref, m_ref):
    gain = 1.0 + mod_ref[5:6, :]

    def mixer_matmul(rows):
        a = jnp.concatenate([a_ref[h, rows, :] for h in range(DIFF_HEADS)], axis=1)
        m_ref[...] = _dot(y_ref[rows, :], wmix_ref[0:SSD_WIDTH, :]) + _dot(a, wmix_ref[SSD_WIDTH:, :])

    def stream_piece(rows):
        local = slice(rows.start % FFN_SUBTILE, rows.start % FFN_SUBTILE + FFN_PIECE)
        t = DEEPNORM_ALPHA * x_ref[rows, :] + gain * m_ref[local, :]
        return _standardize(t) * lng1_ref[...] + lnb1_ref[...]

    _ffn_block(x_ref.shape[0], 2, stream_piece, mixer_matmul,
               mod_ref, win_ref, wout_ref, lng_ref, lnb_ref, o_ref, u_ref, f_ref)


def _ffn_scratch():
    return [pltpu.VMEM((2, FFN_SUBTILE, D_MODEL), BF16), pltpu.VMEM((FFN_SUBTILE, D_MODEL), F32)]


def _ffn_specs(B, L, layer, which):
    tm = min(L, FFN_TOKEN_TILE)

    def tok(width):
        return pl.BlockSpec((None, tm, width), lambda b, i: (b, i, 0))

    mod_spec = pl.BlockSpec((None, 3 * N_SUB, D_MODEL), lambda b, i: (b, 0, 0))
    weights = [_const_spec((D_MODEL, 2 * D_FF), (layer, which)), _const_spec((D_FF, D_MODEL), (layer, which)),
               _const_spec((1, D_MODEL)), _const_spec((1, D_MODEL))]
    return (B, L // tm), tm, tok, mod_spec, weights


def _ffn(x, mod, w_in, w_out, ln_g, ln_b, sub, layer):
    B, L, _ = x.shape
    grid, _, tok, mod_spec, weights = _ffn_specs(B, L, layer, 0)
    return pl.pallas_call(
        functools.partial(_ffn_kernel, sub=sub),
        grid=grid,
        in_specs=[tok(D_MODEL), mod_spec] + weights,
        out_specs=tok(D_MODEL),
        out_shape=jax.ShapeDtypeStruct(x.shape, F32),
        scratch_shapes=_ffn_scratch(),
        compiler_params=_params("arbitrary", "arbitrary"),
        name=f"ffn{sub}",
    )(x, mod, w_in, w_out, ln_g, ln_b)


def _mixer_out_ffn(x, y, a, mod, w_mix, ln_g1, ln_b1, w_in, w_out, ln_g, ln_b, layer):
    B, L, _ = x.shape
    grid, grid_tm, tok, mod_spec, weights = _ffn_specs(B, L, layer, 1)
    return pl.pallas_call(
        _mixer_out_ffn_kernel,
        grid=grid,
        in_specs=[tok(D_MODEL), tok(SSD_WIDTH), _head_major_spec(grid_tm), mod_spec,
                  _const_spec((SSD_WIDTH + DIFF_WIDTH, D_MODEL), (layer,)),
                  _const_spec((1, D_MODEL)), _const_spec((1, D_MODEL))]
        + weights,
        out_specs=tok(D_MODEL),
        out_shape=jax.ShapeDtypeStruct(x.shape, F32),
        scratch_shapes=_ffn_scratch() + [pltpu.VMEM((FFN_SUBTILE, D_MODEL), F32)],
        compiler_params=_params("arbitrary", "arbitrary"),
        name="mixer_out_ffn2",
    )(x, y, a, mod, w_mix, ln_g1, ln_b1, w_in, w_out, ln_g, ln_b)


_Z0 = 0
_XBC0 = SSD_WIDTH
_Q0 = _XBC0 + CONV_CH
_K0 = _Q0 + QK_WIDTH
_V0 = _K0 + QK_WIDTH
_DT0 = _V0 + DIFF_WIDTH


def _inproj_kernel(x_ref, mod_ref, w_ref, cos_ref, sin_ref,
                   z_ref, xbc_ref, q_ref, k_ref, v_ref, dt_ref):
    x = x_ref[...]
    shift = mod_ref[3:4, :]
    scale = mod_ref[4:5, :]
    u = (_standardize(x) * (1.0 + scale) + shift).astype(BF16)
    z_ref[...] = _dot(u, w_ref[:, _Z0:_XBC0])
    xbc_ref[...] = _dot(u, w_ref[:, _XBC0:_Q0])
    vv = _dot(u, w_ref[:, _V0:_DT0]).astype(BF16)
    for h in range(DIFF_HEADS):
        v_ref[h, :, :] = vv[:, h * LANES:(h + 1) * LANES]
    dt_ref[...] = _dot(u, w_ref[:, _DT0:_DT0 + DT_PAD])
    cos = cos_ref[...]
    sin = sin_ref[...]
    d = lax.broadcasted_iota(jnp.int32, cos.shape, 1) & (DIFF_HEAD_DIM - 1)
    first_half = d < ROT_DIM // 2
    half = ROT_DIM // 2
    q_scale = DIFF_HEAD_DIM ** -0.5 * LOG2_E
    for col0, ref, mult in ((_Q0, q_ref, q_scale), (_K0, k_ref, None)):
        for pair in range(DIFF_HEADS // 2):
            lo2 = pair * 2 * LANES
            t2 = _dot(u, w_ref[:, col0 + lo2:col0 + lo2 + 2 * LANES])
            for j in range(2):
                t = t2[:, j * LANES:(j + 1) * LANES]
                partner = jnp.where(first_half, pltpu.roll(t, LANES - half, axis=1), pltpu.roll(t, half, axis=1))
                r = t * cos + partner * sin
                if mult is not None:
                    r = r * mult
                ref[2 * pair + j, :, :] = r.astype(BF16)


def _inproj(x, mod, w, cos, sin, layer):
    B, L, _ = x.shape
    tm = min(L, TOKEN_TILE)

    def tok(width):
        return pl.BlockSpec((None, tm, width), lambda b, i: (b, i, 0))

    return pl.pallas_call(
        _inproj_kernel,
        grid=(B, L // tm),
        in_specs=[
            tok(D_MODEL),
            pl.BlockSpec((None, 3 * N_SUB, D_MODEL), lambda b, i: (b, 0, 0)),
            _const_spec((D_MODEL, PROJ_COLS), (layer,)),
            tok(LANES), tok(LANES),
        ],
        out_specs=[tok(SSD_WIDTH), tok(CONV_CH), _head_major_spec(tm), _head_major_spec(tm), _head_major_spec(tm),
                   tok(DT_PAD)],
        out_shape=[
            jax.ShapeDtypeStruct((B, L, SSD_WIDTH), F32),
            jax.ShapeDtypeStruct((B, L, CONV_CH), F32),
            jax.ShapeDtypeStruct((B, DIFF_HEADS, L, LANES), BF16),
            jax.ShapeDtypeStruct((B, DIFF_HEADS, L, LANES), BF16),
            jax.ShapeDtypeStruct((B, DIFF_HEADS, L, LANES), BF16),
            jax.ShapeDtypeStruct((B, L, DT_PAD), F32),
        ],
        compiler_params=_params("arbitrary", "arbitrary"),
        name="mixer_inproj",
    )(x, mod, w, cos, sin)


def _ssd_kernel(xbc_ref, dt_ref, z_ref, convw_ref, convb_ref, dtb_ref, alog_ref, dskip_ref, normw_ref,
                sel3_ref, tri3_ref, y_ref, xpad_ref, state_ref, *, tl):
    halo = SUBLANES
    hg = SSD_HEADS // SSD_GROUPS
    gw = hg * SSD_HEAD_DIM

    @pl.when(pl.program_id(1) == 0)
    def _():
        xpad_ref[0:halo, :] = jnp.zeros((halo, CONV_CH), F32)
        state_ref[...] = jnp.zeros(state_ref.shape, F32)

    xpad_ref[halo:halo + tl, :] = xbc_ref[...]
    xpad = xpad_ref[...]
    conv = convb_ref[...] + convw_ref[SSD_CONV - 1:SSD_CONV, :] * xpad[halo:halo + tl, :]
    for back in range(1, SSD_CONV):
        k = SSD_CONV - 1 - back
        conv = conv + convw_ref[k:k + 1, :] * pltpu.roll(xpad, back, axis=0)[halo:halo + tl, :]
    xpad_ref[0:halo, :] = xpad_ref[tl:tl + halo, :]
    act = _silu(conv)
    xs = act[:, :SSD_WIDTH]
    bm = act[:, SSD_WIDTH:SSD_WIDTH + SSD_GROUPS * SSD_STATE].astype(BF16)
    cm = act[:, SSD_WIDTH + SSD_GROUPS * SSD_STATE:].astype(BF16)

    lane = lax.broadcasted_iota(jnp.int32, (1, LANES), 1)
    v = dt_ref[...] + dtb_ref[...]
    dt = jnp.maximum(v, 0.0) + jnp.log1p(jnp.exp(-jnp.abs(v)))
    a = jnp.where(lane < SSD_HEADS, dt * (-jnp.exp(alog_ref[...])), 0.0)

    li = lax.broadcasted_iota(jnp.int32, (SSD_Q, SSD_Q), 0)
    si = lax.broadcasted_iota(jnp.int32, (SSD_Q, SSD_Q), 1)
    causal = li >= si
    sel3 = sel3_ref[...]
    tri3 = tri3_ref[...]

    xdt = xs * _select_exact(dt, sel3_ref[:, :SSD_WIDTH])
    lo_half = lax.broadcasted_iota(jnp.int32, (1, LANES), 1) < SSD_HEAD_DIM

    y_blocks = []
    for c in range(tl // SSD_Q):
        r0 = c * SSD_Q
        acs = _prefix_sum_exact(tri3, a[r0:r0 + SSD_Q, :])
        acs_t = acs.T
        acs_wide = _select_exact(acs, sel3)
        acs_head = acs_wide[:, :SSD_WIDTH]
        acs_tile = acs_wide[:, SSD_WIDTH:]
        last_head = acs_head[SSD_Q - 1:SSD_Q, :]
        grow = jnp.exp(acs_head)
        to_end = jnp.exp(last_head - acs_head)
        xdt_c = xdt[r0:r0 + SSD_Q, :]
        y_groups = []
        for g in range(SSD_GROUPS):
            bg = bm[r0:r0 + SSD_Q, g * SSD_STATE:(g + 1) * SSD_STATE]
            cg = cm[r0:r0 + SSD_Q, g * SSD_STATE:(g + 1) * SSD_STATE]
            cb = _dot_nt(cg, bg)
            st = state_ref[g]
            y_off = _dot_nt(cg, st.astype(BF16))
            y_pairs = []
            for p in range(hg // 2):
                ms, rhs = [], []
                xp = xdt_c[:, g * gw + p * LANES:g * gw + (p + 1) * LANES].astype(BF16)
                for j in range(2):
                    h = g * hg + 2 * p + j
                    seg = acs_tile[:, h * LANES:(h + 1) * LANES] - acs_t[h:h + 1, :]
                    decay = jnp.exp(jnp.where(causal, seg, -jnp.inf))
                    ms.append((cb * decay).astype(BF16))
                    keep = lo_half if j == 0 else jnp.logical_not(lo_half)
                    rhs.append(jnp.where(keep, xp, jnp.zeros_like(xp)))
                y_pairs.append(_dot(jnp.concatenate(ms, axis=1), jnp.concatenate(rhs, axis=0)))
            y_diag = jnp.concatenate(y_pairs, axis=1)
            y_groups.append(y_diag + y_off * grow[:, g * gw:(g + 1) * gw])
            wgt = (xdt_c[:, g * gw:(g + 1) * gw] * to_end[:, g * gw:(g + 1) * gw]).astype(BF16)
            d_state = _dot_tn(wgt, bg)
            scale_rows = []
            for j in range(hg):
                h = g * hg + j
                scale_rows.append(jnp.broadcast_to(jnp.exp(acs_t[h:h + 1, SSD_Q - 1:SSD_Q]), (SSD_HEAD_DIM, SSD_STATE)))
            state_ref[g] = st * jnp.concatenate(scale_rows, axis=0) + d_state
        y_blocks.append(jnp.concatenate(y_groups, axis=1))
    y = jnp.concatenate(y_blocks, axis=0) if len(y_blocks) > 1 else y_blocks[0]
    y = y + xs * dskip_ref[...]
    y = y * _silu(z_ref[...])
    outs = []
    for g in range(SSD_GROUPS):
        yg = y[:, g * gw:(g + 1) * gw]
        ms = jnp.mean(yg * yg, axis=-1, keepdims=True)
        outs.append(yg * lax.rsqrt(ms + LN_EPS) * normw_ref[:, g * gw:(g + 1) * gw])
    y_ref[...] = jnp.concatenate(outs, axis=1).astype(BF16)


def _ssd_constants():
    wide = SSD_WIDTH + SSD_HEADS * LANES
    row = np.arange(3 * LANES)[:, None] % LANES
    col = np.arange(wide)[None, :]
    col_head = np.where(col < SSD_WIDTH, col // SSD_HEAD_DIM, (col - SSD_WIDTH) // LANES)
    sel3 = (row == col_head).astype(np.float32)
    li = np.arange(SSD_Q)[:, None]
    si = np.arange(3 * SSD_Q)[None, :] % SSD_Q
    tri3 = (li >= si).astype(np.float32)
    return jnp.asarray(sel3, BF16), jnp.asarray(tri3, BF16)


def _ssd(xbc, dt_raw, z, conv_w, conv_b, dt_bias, a_log, d_skip, norm_w):
    B, L, _ = xbc.shape
    tl = min(L, SSD_TILE)
    sel3, tri3 = _ssd_constants()

    def tok(width):
        return pl.BlockSpec((None, tl, width), lambda b, i: (b, i, 0))

    def pad_heads(p):
        return jnp.zeros((1, DT_PAD), F32).at[0, :SSD_HEADS].set(p)

    return pl.pallas_call(
        functools.partial(_ssd_kernel, tl=tl),
        grid=(B, L // tl),
        in_specs=[
            tok(CONV_CH), tok(DT_PAD), tok(SSD_WIDTH),
            _const_spec((SSD_CONV, CONV_CH)), _const_spec((1, CONV_CH)),
            _const_spec((1, DT_PAD)), _const_spec((1, DT_PAD)),
            _const_spec((1, SSD_WIDTH)), _const_spec((1, SSD_WIDTH)),
            _const_spec(sel3.shape), _const_spec(tri3.shape),
        ],
        out_specs=tok(SSD_WIDTH),
        out_shape=jax.ShapeDtypeStruct((B, L, SSD_WIDTH), BF16),
        scratch_shapes=[
            pltpu.VMEM((tl + SUBLANES, CONV_CH), F32),
            pltpu.VMEM((SSD_GROUPS, (SSD_HEADS // SSD_GROUPS) * SSD_HEAD_DIM, SSD_STATE), F32),
        ],
        compiler_params=_params("arbitrary", "arbitrary"),
        name="ssd_scan",
    )(xbc, dt_raw, z, conv_w, conv_b.reshape(1, CONV_CH), pad_heads(dt_bias), pad_heads(a_log),
      jnp.repeat(d_skip, SSD_HEAD_DIM).reshape(1, SSD_WIDTH), norm_w.reshape(1, SSD_WIDTH), sel3, tri3)


def _attn_kernel(lam_ref, q_ref, k_ref, v_ref, subw_ref, o_ref, s_ref, e_ref, vx_ref, *, seq, tq, lam_init):
    rows = 2 * tq
    vx_ref[:, 0:DIFF_V_DIM] = v_ref[...]
    vx_ref[:, DIFF_V_DIM:] = jnp.ones((seq, LANES), BF16)
    lp = lam_ref[...]
    lam = (jnp.exp(jnp.sum(lp[0:1] * lp[1:2], axis=-1, keepdims=True))
           - jnp.exp(jnp.sum(lp[2:3] * lp[3:4], axis=-1, keepdims=True)) + lam_init)
    lo_half = lax.broadcasted_iota(jnp.int32, (1, LANES), 1) < DIFF_HEAD_DIM
    qchunk = (lax.broadcasted_iota(jnp.int32, (rows, tq), 0) & (tq - 1)) >> 6
    kchunk = lax.broadcasted_iota(jnp.int32, (rows, tq), 1) >> 6
    allowed = kchunk <= qchunk
    for qi in reversed(range(seq // tq)):
        r0 = qi * tq
        ext = r0 + tq
        q = q_ref[r0:r0 + tq, :]
        zero = jnp.zeros_like(q)
        q2 = jnp.concatenate([jnp.where(lo_half, q, zero), jnp.where(lo_half, zero, q)], axis=0)
        blocks = [(c0, min(ATTN_KEY_BLOCK, r0 - c0)) for c0 in range(0, r0, ATTN_KEY_BLOCK)] + [(r0, tq)]
        for c0, cw in blocks:
            sc = _dot_nt(q2, k_ref[c0:c0 + cw, :])
            if c0 == r0:
                sc = jnp.where(allowed, sc, -jnp.inf)
            s_ref[:, c0:c0 + cw] = sc
        for rg in range(rows // ATTN_ROW_GROUP):
            rsl = slice(rg * ATTN_ROW_GROUP, (rg + 1) * ATTN_ROW_GROUP)
            mx = s_ref[rsl, 0:LANES]
            for j in range(1, ext // LANES):
                mx = jnp.maximum(mx, s_ref[rsl, j * LANES:(j + 1) * LANES])
            mb = jnp.broadcast_to(jnp.max(mx, axis=-1, keepdims=True), (ATTN_ROW_GROUP, LANES))
            for j in range(ext // LANES):
                csl = slice(j * LANES, (j + 1) * LANES)
                e_ref[rsl, csl] = jnp.exp2(s_ref[rsl, csl] - mb).astype(BF16)
        acc = _dot(e_ref[:, 0:ext], vx_ref[0:ext, :])
        inv = 1.0 / acc[:, DIFF_V_DIM:]
        acc = acc[:, :DIFF_V_DIM]
        o = acc[:tq] * inv[:tq] - acc[tq:] * (lam * inv[tq:])
        o = o * lax.rsqrt(jnp.mean(o * o, axis=-1, keepdims=True) + LN_EPS) * subw_ref[...] * (1.0 - lam_init)
        o_ref[r0:r0 + tq, :] = o.astype(BF16)


def _attention(q, k, v, diff_lambda, subln_w, lam_init):
    B, _, L, _ = q.shape
    tq = min(L, ATTN_Q_TILE)
    head = pl.BlockSpec((None, None, L, LANES), lambda b, h: (b, h, 0, 0))
    return pl.pallas_call(
        functools.partial(_attn_kernel, seq=L, tq=tq, lam_init=lam_init),
        grid=(B, DIFF_HEADS),
        in_specs=[_const_spec((4, DIFF_HEAD_DIM)), head, head, head, _const_spec((1, DIFF_V_DIM))],
        out_specs=head,
        out_shape=jax.ShapeDtypeStruct((B, DIFF_HEADS, L, LANES), BF16),
        scratch_shapes=[
            pltpu.VMEM((2 * tq, L), F32),
            pltpu.VMEM((2 * tq, L), BF16),
            pltpu.VMEM((L, DIFF_V_DIM + LANES), BF16),
        ],
        compiler_params=_params("arbitrary", "arbitrary"),
        name="diff_attention",
    )(diff_lambda, q, k, v, subln_w.reshape(1, DIFF_V_DIM))


def _pack_w_in(w):
    z, xbc, dt, q, k, v = jnp.split(w, (512, 1536, 1544, 2056, 2568), axis=-1)
    dt = jnp.pad(dt, ((0, 0),) * (w.ndim - 1) + ((0, DT_PAD - SSD_HEADS),))
    return jnp.concatenate([z, xbc, q, k, v, dt], axis=-1).astype(BF16)


def kernel(x, c, positions, w_ada, b_ada, w_ffn_in, w_ffn_out, w_in, conv_w, conv_b, dt_bias, a_log,
           d_skip, ssd_norm_w, diff_lambda, subln_w, w_out, ln_g, ln_b):
    B = x.shape[0]
    mod_all = _ada_mod(c, w_ada, b_ada).reshape(DEPTH, B, 3 * N_SUB, D_MODEL)
    cos, sin = _rope_tables(positions)
    w_ffn_in, w_ffn_out, w_out = (w.astype(BF16) for w in (w_ffn_in, w_ffn_out, w_out))
    w_proj = _pack_w_in(w_in)
    for layer in range(DEPTH):
        mod = mod_all[layer]
        lam_init = 0.8 - 0.6 * float(np.exp(-0.3 * layer))
        g = ln_g[layer].reshape(N_SUB, 1, D_MODEL)
        b = ln_b[layer].reshape(N_SUB, 1, D_MODEL)
        x = _ffn(x, mod, w_ffn_in, w_ffn_out, g[0], b[0], 0, layer)
        z, xbc, q, k, v, dt_raw = _inproj(x, mod, w_proj, cos, sin, layer)
        y = _ssd(xbc, dt_raw, z, conv_w[layer], conv_b[layer], dt_bias[layer], a_log[layer],
                 d_skip[layer], ssd_norm_w[layer])
        o = _attention(q, k, v, diff_lambda[layer], subln_w[layer], lam_init)
        x = _mixer_out_ffn(x, y, o, mod, w_out, g[1], b[1], w_ffn_in, w_ffn_out, g[2], b[2], layer)
    return x
```

```python
import functools

import numpy as np
import jax
import jax.numpy as jnp
from jax import lax
from jax.experimental import pallas as pl
from jax.experimental.pallas import tpu as pltpu

F32 = jnp.float32
BF16 = jnp.bfloat16

D_MODEL = 1024
DEPTH = 4
CHUNK = 64
SSD_HEADS = 8
SSD_HEAD_DIM = 64
SSD_WIDTH = 512
SSD_GROUPS = 2
SSD_STATE = 128
SSD_CONV = 4
DIFF_HEADS = 4
DIFF_HEAD_DIM = 64
DIFF_V_DIM = 128
DIFF_WIDTH = 512
ROT_DIM = 16
ROPE_THETA = 500000.0
CONV_CH = 1024
QK_WIDTH = 512
D_FF = 2816
N_SUB = 3
LN_EPS = 1e-5
DEEPNORM_ALPHA = (2 * DEPTH) ** 0.25
FFN_RES_WEIGHT = 0.5
LOG2_E = 1.4426950408889634

LANES = 128
SUBLANES = 8
VMEM_LIMIT = 56 * 1024 * 1024

TOKEN_TILE = 512
FFN_TOKEN_TILE = 1024
FFN_SUBTILE = 512
FFN_PIECE = 64
FFN_CHUNKS = 11
ATTN_Q_TILE = 256
ATTN_KEY_BLOCK = 512
ATTN_ROW_GROUP = 64
SSD_TILE = 1024
SSD_Q = 128
DT_PAD = LANES
PROJ_COLS = SSD_WIDTH + CONV_CH + 2 * QK_WIDTH + DIFF_WIDTH + DT_PAD


def _params(*sem):
    return pltpu.CompilerParams(dimension_semantics=sem, vmem_limit_bytes=VMEM_LIMIT)


def _const_spec(shape, lead=()):
    nd = len(shape)
    return pl.BlockSpec((None,) * len(lead) + tuple(shape), lambda *_: tuple(lead) + (0,) * nd,
                        pipeline_mode=pl.Buffered(1))


def _head_major_spec(tm):
    return pl.BlockSpec((None, DIFF_HEADS, tm, LANES), lambda b, i: (b, 0, i, 0))


def _standardize(x):
    mu = jnp.mean(x, axis=-1, keepdims=True)
    xc = x - mu
    var = jnp.mean(xc * xc, axis=-1, keepdims=True)
    return xc * lax.rsqrt(var + LN_EPS)


def _silu(x):
    h = 0.5 * x
    return h + h * jnp.tanh(h)


def _dot(a, b):
    return jnp.dot(a, b, preferred_element_type=F32)


def _dot_nt(a, b):
    return lax.dot_general(a, b, (((1,), (1,)), ((), ())), preferred_element_type=F32)


def _dot_tn(a, b):
    return lax.dot_general(a, b, (((0,), (0,)), ((), ())), preferred_element_type=F32)


def _split3(x):
    hi = x.astype(BF16)
    r1 = x - hi.astype(F32)
    mid = r1.astype(BF16)
    lo = (r1 - mid.astype(F32)).astype(BF16)
    return hi, mid, lo


def _select_exact(x, sel3):
    return _dot(jnp.concatenate(_split3(x), axis=1), sel3)


def _prefix_sum_exact(tri3, x):
    return _dot(tri3, jnp.concatenate(_split3(x), axis=0))


def _ada_kernel(c_ref, w_ref, b_ref, o_ref):
    sc = _silu(c_ref[...]).astype(BF16)
    o_ref[...] = _dot(sc, w_ref[...].astype(BF16)) + b_ref[...]


def _ada_mod(c, w_ada, b_ada):
    B = c.shape[0]
    ncol = w_ada.shape[-1]
    tn = 1024
    return pl.pallas_call(
        _ada_kernel,
        grid=(DEPTH, ncol // tn),
        in_specs=[
            pl.BlockSpec((B, D_MODEL), lambda l, j: (0, 0)),
            pl.BlockSpec((None, D_MODEL, tn), lambda l, j: (l, 0, j)),
            pl.BlockSpec((None, 1, tn), lambda l, j: (l, 0, j)),
        ],
        out_specs=pl.BlockSpec((None, B, tn), lambda l, j: (l, 0, j)),
        out_shape=jax.ShapeDtypeStruct((DEPTH, B, ncol), F32),
        compiler_params=_params("arbitrary", "arbitrary"),
        name="ada_mod",
    )(c, w_ada, b_ada.reshape(DEPTH, 1, ncol))


def _rope_kernel(pos_ref, inv_ref, cos_ref, sin_ref):
    ang = pos_ref[...].astype(F32) * inv_ref[...]
    d = lax.broadcasted_iota(jnp.int32, ang.shape, 1) & (DIFF_HEAD_DIM - 1)
    c = jnp.cos(ang)
    s = jnp.sin(ang)
    cos_ref[...] = jnp.where(d < ROT_DIM, c, 1.0)
    sin_ref[...] = jnp.where(d < ROT_DIM // 2, -s, jnp.where(d < ROT_DIM, s, 0.0))


def _rope_tables(positions):
    B, L = positions.shape
    half = ROT_DIM // 2
    inv = np.float32(ROPE_THETA) ** (-np.arange(half, dtype=np.float32) * np.float32(2.0) / np.float32(ROT_DIM))
    inv_lane = jnp.asarray(np.tile(inv.astype(np.float32), LANES // half)[None, :])
    tm = min(L, 1024)
    spec = pl.BlockSpec((None, tm, LANES), lambda b, i: (b, i, 0))
    return pl.pallas_call(
        _rope_kernel,
        grid=(B, L // tm),
        in_specs=[pl.BlockSpec((None, tm, 1), lambda b, i: (b, i, 0)),
                  pl.BlockSpec((1, LANES), lambda b, i: (0, 0))],
        out_specs=[spec, spec],
        out_shape=[jax.ShapeDtypeStruct((B, L, LANES), F32)] * 2,
        compiler_params=_params("arbitrary", "arbitrary"),
        name="rope_tables",
    )(positions.reshape(B, L, 1), inv_lane)


def _zero_after(x, width):
    bits = pltpu.bitcast(x, jnp.int32)
    acc = bits[0:SUBLANES, :]
    for r in range(SUBLANES, x.shape[0], SUBLANES):
        acc = acc | bits[r:r + SUBLANES, :]
    lane = acc[:, 0:LANES]
    for c in range(LANES, x.shape[1], LANES):
        lane = lane | acc[:, c:c + LANES]
    sixteen = jnp.full(lane.shape, 16, jnp.int32)
    zero = lax.shift_right_logical(lax.shift_right_logical(lane, sixteen), sixteen).astype(F32)
    return jnp.concatenate([zero] * (width // LANES), axis=1)


def _ffn_block(n_rows, sub, stream_piece, next_tile_matmul, mod_ref, win_ref, wout_ref, lng_ref, lnb_ref,
               o_ref, u_ref, f_ref):
    shift = mod_ref[3 * sub:3 * sub + 1, :]
    scale1 = 1.0 + mod_ref[3 * sub + 1:3 * sub + 2, :]
    res_gain = FFN_RES_WEIGHT * (1.0 + mod_ref[3 * sub + 2:3 * sub + 3, :])
    tiles = n_rows // FFN_SUBTILE
    pieces = FFN_SUBTILE // FFN_PIECE
    fc = D_FF // FFN_CHUNKS
    assert pieces <= FFN_CHUNKS

    def tile_rows(t):
        return slice(t * FFN_SUBTILE, (t + 1) * FFN_SUBTILE)

    def modulate_piece(t, p):
        rows = slice(t * FFN_SUBTILE + p * FFN_PIECE, t * FFN_SUBTILE + (p + 1) * FFN_PIECE)
        s = stream_piece(rows)
        o_ref[rows, :] = s
        u = _standardize(s) * scale1 + shift
        u_ref[t % 2, p * FFN_PIECE:(p + 1) * FFN_PIECE, :] = u.astype(BF16)
        return u

    def norm_piece(t, p):
        rows = slice(t * FFN_SUBTILE + p * FFN_PIECE, t * FFN_SUBTILE + (p + 1) * FFN_PIECE)
        y = DEEPNORM_ALPHA * o_ref[rows, :] + res_gain * f_ref[p * FFN_PIECE:(p + 1) * FFN_PIECE, :]
        o_ref[rows, :] = _standardize(y) * lng_ref[...] + lnb_ref[...]

    if next_tile_matmul is not None:
        next_tile_matmul(tile_rows(0))
    for p in range(pieces):
        modulate_piece(0, p)
    for t in range(tiles):
        if next_tile_matmul is not None and t + 1 < tiles:
            next_tile_matmul(tile_rows(t + 1))
        f = None
        pending = None
        for k in range(FFN_CHUNKS):
            u = u_ref[t % 2]
            g = _dot(u, win_ref[:, k * fc:(k + 1) * fc])
            up = _dot(u, win_ref[:, D_FF + k * fc:D_FF + (k + 1) * fc])
            if pending is not None:
                up = jnp.concatenate([up[0:SUBLANES] + _zero_after(pending, fc), up[SUBLANES:]], axis=0)
                pending = None
            h = (_silu(g) * up).astype(BF16)
            fk = _dot(h, wout_ref[k * fc:(k + 1) * fc, :])
            f = fk if f is None else f + fk
            if k < pieces:
                if t + 1 < tiles:
                    pending = modulate_piece(t + 1, k)
                if t >= 1:
                    norm_piece(t - 1, k)
        f_ref[...] = f
    for p in range(pieces):
        norm_piece(tiles - 1, p)


def _ffn_kernel(x_ref, mod_ref, win_ref, wout_ref, lng_ref, lnb_ref, o_ref, u_ref, f_ref, *, sub):
    _ffn_block(x_ref.shape[0], sub, lambda rows: x_ref[rows, :], None,
               mod_ref, win_ref, wout_ref, lng_ref, lnb_ref, o_ref, u_ref, f_ref)


def _mixer_out_ffn_kernel(x_ref, y_ref, a_ref, mod_ref, wmix_ref, lng1_ref, lnb1_ref,
                          win_ref, wout_ref, lng_ref, lnb_ref, o_ref, u_ref, f_ref, m_ref):
    gain = 1.0 + mod_ref[5:6, :]

    def mixer_matmul(rows):
        a = jnp.concatenate([a_ref[h, rows, :] for h in range(DIFF_HEADS)], axis=1)
        m_ref[...] = _dot(y_ref[rows, :], wmix_ref[0:SSD_WIDTH, :]) + _dot(a, wmix_ref[SSD_WIDTH:, :])

    def stream_piece(rows):
        local = slice(rows.start % FFN_SUBTILE, rows.start % FFN_SUBTILE + FFN_PIECE)
        t = DEEPNORM_ALPHA * x_ref[rows, :] + gain * m_ref[local, :]
        return _standardize(t) * lng1_ref[...] + lnb1_ref[...]

    _ffn_block(x_ref.shape[0], 2, stream_piece, mixer_matmul,
               mod_ref, win_ref, wout_ref, lng_ref, lnb_ref, o_ref, u_ref, f_ref)


def _ffn_scratch():
    return [pltpu.VMEM((2, FFN_SUBTILE, D_MODEL), BF16), pltpu.VMEM((FFN_SUBTILE, D_MODEL), F32)]


def _ffn_specs(B, L, layer, which):
    tm = min(L, FFN_TOKEN_TILE)

    def tok(width):
        return pl.BlockSpec((None, tm, width), lambda b, i: (b, i, 0))

    mod_spec = pl.BlockSpec((None, 3 * N_SUB, D_MODEL), lambda b, i: (b, 0, 0))
    weights = [_const_spec((D_MODEL, 2 * D_FF), (layer, which)), _const_spec((D_FF, D_MODEL), (layer, which)),
               _const_spec((1, D_MODEL)), _const_spec((1, D_MODEL))]
    return (B, L // tm), tm, tok, mod_spec, weights


def _ffn(x, mod, w_in, w_out, ln_g, ln_b, sub, layer):
    B, L, _ = x.shape
    grid, _, tok, mod_spec, weights = _ffn_specs(B, L, layer, 0)
    return pl.pallas_call(
        functools.partial(_ffn_kernel, sub=sub),
        grid=grid,
        in_specs=[tok(D_MODEL), mod_spec] + weights,
        out_specs=tok(D_MODEL),
        out_shape=jax.ShapeDtypeStruct(x.shape, F32),
        scratch_shapes=_ffn_scratch(),
        compiler_params=_params("arbitrary", "arbitrary"),
        name=f"ffn{sub}",
    )(x, mod, w_in, w_out, ln_g, ln_b)


def _mixer_out_ffn(x, y, a, mod, w_mix, ln_g1, ln_b1, w_in, w_out, ln_g, ln_b, layer):
    B, L, _ = x.shape
    grid, grid_tm, tok, mod_spec, weights = _ffn_specs(B, L, layer, 1)
    return pl.pallas_call(
        _mixer_out_ffn_kernel,
        grid=grid,
        in_specs=[tok(D_MODEL), tok(SSD_WIDTH), _head_major_spec(grid_tm), mod_spec,
                  _const_spec((SSD_WIDTH + DIFF_WIDTH, D_MODEL), (layer,)),
                  _const_spec((1, D_MODEL)), _const_spec((1, D_MODEL))]
        + weights,
        out_specs=tok(D_MODEL),
        out_shape=jax.ShapeDtypeStruct(x.shape, F32),
        scratch_shapes=_ffn_scratch() + [pltpu.VMEM((FFN_SUBTILE, D_MODEL), F32)],
        compiler_params=_params("arbitrary", "arbitrary"),
        name="mixer_out_ffn2",
    )(x, y, a, mod, w_mix, ln_g1, ln_b1, w_in, w_out, ln_g, ln_b)


_Z0 = 0
_XBC0 = SSD_WIDTH
_Q0 = _XBC0 + CONV_CH
_K0 = _Q0 + QK_WIDTH
_V0 = _K0 + QK_WIDTH
_DT0 = _V0 + DIFF_WIDTH


def _inproj_kernel(x_ref, mod_ref, w_ref, cos_ref, sin_ref,
                   z_ref, xbc_ref, q_ref, k_ref, v_ref, dt_ref):
    x = x_ref[...]
    shift = mod_ref[3:4, :]
    scale = mod_ref[4:5, :]
    u = (_standardize(x) * (1.0 + scale) + shift).astype(BF16)
    z_ref[...] = _dot(u, w_ref[:, _Z0:_XBC0])
    xbc_ref[...] = _dot(u, w_ref[:, _XBC0:_Q0])
    vv = _dot(u, w_ref[:, _V0:_DT0]).astype(BF16)
    for h in range(DIFF_HEADS):
        v_ref[h, :, :] = vv[:, h * LANES:(h + 1) * LANES]
    dt_ref[...] = _dot(u, w_ref[:, _DT0:_DT0 + DT_PAD])
    cos = cos_ref[...]
    sin = sin_ref[...]
    d = lax.broadcasted_iota(jnp.int32, cos.shape, 1) & (DIFF_HEAD_DIM - 1)
    first_half = d < ROT_DIM // 2
    half = ROT_DIM // 2
    q_scale = DIFF_HEAD_DIM ** -0.5 * LOG2_E
    for col0, ref, mult in ((_Q0, q_ref, q_scale), (_K0, k_ref, None)):
        for pair in range(DIFF_HEADS // 2):
            lo2 = pair * 2 * LANES
            t2 = _dot(u, w_ref[:, col0 + lo2:col0 + lo2 + 2 * LANES])
            for j in range(2):
                t = t2[:, j * LANES:(j + 1) * LANES]
                partner = jnp.where(first_half, pltpu.roll(t, LANES - half, axis=1), pltpu.roll(t, half, axis=1))
                r = t * cos + partner * sin
                if mult is not None:
                    r = r * mult
                ref[2 * pair + j, :, :] = r.astype(BF16)


def _inproj(x, mod, w, cos, sin, layer):
    B, L, _ = x.shape
    tm = min(L, TOKEN_TILE)

    def tok(width):
        return pl.BlockSpec((None, tm, width), lambda b, i: (b, i, 0))

    return pl.pallas_call(
        _inproj_kernel,
        grid=(B, L // tm),
        in_specs=[
            tok(D_MODEL),
            pl.BlockSpec((None, 3 * N_SUB, D_MODEL), lambda b, i: (b, 0, 0)),
            _const_spec((D_MODEL, PROJ_COLS), (layer,)),
            tok(LANES), tok(LANES),
        ],
        out_specs=[tok(SSD_WIDTH), tok(CONV_CH), _head_major_spec(tm), _head_major_spec(tm), _head_major_spec(tm),
                   tok(DT_PAD)],
        out_shape=[
            jax.ShapeDtypeStruct((B, L, SSD_WIDTH), F32),
            jax.ShapeDtypeStruct((B, L, CONV_CH), F32),
            jax.ShapeDtypeStruct((B, DIFF_HEADS, L, LANES), BF16),
            jax.ShapeDtypeStruct((B, DIFF_HEADS, L, LANES), BF16),
            jax.ShapeDtypeStruct((B, DIFF_HEADS, L, LANES), BF16),
            jax.ShapeDtypeStruct((B, L, DT_PAD), F32),
        ],
        compiler_params=_params("arbitrary", "arbitrary"),
        name="mixer_inproj",
    )(x, mod, w, cos, sin)


def _ssd_kernel(xbc_ref, dt_ref, z_ref, convw_ref, convb_ref, dtb_ref, alog_ref, dskip_ref, normw_ref,
                sel3_ref, tri3_ref, y_ref, xpad_ref, state_ref, *, tl):
    halo = SUBLANES
    hg = SSD_HEADS // SSD_GROUPS
    gw = hg * SSD_HEAD_DIM

    @pl.when(pl.program_id(1) == 0)
    def _():
        xpad_ref[0:halo, :] = jnp.zeros((halo, CONV_CH), F32)
        state_ref[...] = jnp.zeros(state_ref.shape, F32)

    xpad_ref[halo:halo + tl, :] = xbc_ref[...]
    xpad = xpad_ref[...]
    conv = convb_ref[...] + convw_ref[SSD_CONV - 1:SSD_CONV, :] * xpad[halo:halo + tl, :]
    for back in range(1, SSD_CONV):
        k = SSD_CONV - 1 - back
        conv = conv + convw_ref[k:k + 1, :] * pltpu.roll(xpad, back, axis=0)[halo:halo + tl, :]
    xpad_ref[0:halo, :] = xpad_ref[tl:tl + halo, :]
    act = _silu(conv)
    xs = act[:, :SSD_WIDTH]
    bm = act[:, SSD_WIDTH:SSD_WIDTH + SSD_GROUPS * SSD_STATE].astype(BF16)
    cm = act[:, SSD_WIDTH + SSD_GROUPS * SSD_STATE:].astype(BF16)

    lane = lax.broadcasted_iota(jnp.int32, (1, LANES), 1)
    v = dt_ref[...] + dtb_ref[...]
    dt = jnp.maximum(v, 0.0) + jnp.log1p(jnp.exp(-jnp.abs(v)))
    a = jnp.where(lane < SSD_HEADS, dt * (-jnp.exp(alog_ref[...])), 0.0)

    li = lax.broadcasted_iota(jnp.int32, (SSD_Q, SSD_Q), 0)
    si = lax.broadcasted_iota(jnp.int32, (SSD_Q, SSD_Q), 1)
    causal = li >= si
    sel3 = sel3_ref[...]
    tri3 = tri3_ref[...]

    xdt = xs * _select_exact(dt, sel3_ref[:, :SSD_WIDTH])
    lo_half = lax.broadcasted_iota(jnp.int32, (1, LANES), 1) < SSD_HEAD_DIM

    y_blocks = []
    for c in range(tl // SSD_Q):
        r0 = c * SSD_Q
        acs = _prefix_sum_exact(tri3, a[r0:r0 + SSD_Q, :])
        acs_t = acs.T
        acs_wide = _select_exact(acs, sel3)
        acs_head = acs_wide[:, :SSD_WIDTH]
        acs_tile = acs_wide[:, SSD_WIDTH:]
        last_head = acs_head[SSD_Q - 1:SSD_Q, :]
        grow = jnp.exp(acs_head)
        to_end = jnp.exp(last_head - acs_head)
        xdt_c = xdt[r0:r0 + SSD_Q, :]
        y_groups = []
        for g in range(SSD_GROUPS):
            bg = bm[r0:r0 + SSD_Q, g * SSD_STATE:(g + 1) * SSD_STATE]
            cg = cm[r0:r0 + SSD_Q, g * SSD_STATE:(g + 1) * SSD_STATE]
            cb = _dot_nt(cg, bg)
            st = state_ref[g]
            y_off = _dot_nt(cg, st.astype(BF16))
            y_pairs = []
            for p in range(hg // 2):
                ms, rhs = [], []
                xp = xdt_c[:, g * gw + p * LANES:g * gw + (p + 1) * LANES].astype(BF16)
                for j in range(2):
                    h = g * hg + 2 * p + j
                    seg = acs_tile[:, h * LANES:(h + 1) * LANES] - acs_t[h:h + 1, :]
                    decay = jnp.exp(jnp.where(causal, seg, -jnp.inf))
                    ms.append((cb * decay).astype(BF16))
                    keep = lo_half if j == 0 else jnp.logical_not(lo_half)
                    rhs.append(jnp.where(keep, xp, jnp.zeros_like(xp)))
                y_pairs.append(_dot(jnp.concatenate(ms, axis=1), jnp.concatenate(rhs, axis=0)))
            y_diag = jnp.concatenate(y_pairs, axis=1)
            y_groups.append(y_diag + y_off * grow[:, g * gw:(g + 1) * gw])
            wgt = (xdt_c[:, g * gw:(g + 1) * gw] * to_end[:, g * gw:(g + 1) * gw]).astype(BF16)
            d_state = _dot_tn(wgt, bg)
            scale_rows = []
            for j in range(hg):
                h = g * hg + j
                scale_rows.append(jnp.broadcast_to(jnp.exp(acs_t[h:h + 1, SSD_Q - 1:SSD_Q]), (SSD_HEAD_DIM, SSD_STATE)))
            state_ref[g] = st * jnp.concatenate(scale_rows, axis=0) + d_state
        y_blocks.append(jnp.concatenate(y_groups, axis=1))
    y = jnp.concatenate(y_blocks, axis=0) if len(y_blocks) > 1 else y_blocks[0]
    y = y + xs * dskip_ref[...]
    y = y * _silu(z_ref[...])
    outs = []
    for g in range(SSD_GROUPS):
        yg = y[:, g * gw:(g + 1) * gw]
        ms = jnp.mean(yg * yg, axis=-1, keepdims=True)
        outs.append(yg * lax.rsqrt(ms + LN_EPS) * normw_ref[:, g * gw:(g + 1) * gw])
    y_ref[...] = jnp.concatenate(outs, axis=1).astype(BF16)


def _ssd_constants():
    wide = SSD_WIDTH + SSD_HEADS * LANES
    row = np.arange(3 * LANES)[:, None] % LANES
    col = np.arange(wide)[None, :]
    col_head = np.where(col < SSD_WIDTH, col // SSD_HEAD_DIM, (col - SSD_WIDTH) // LANES)
    sel3 = (row == col_head).astype(np.float32)
    li = np.arange(SSD_Q)[:, None]
    si = np.arange(3 * SSD_Q)[None, :] % SSD_Q
    tri3 = (li >= si).astype(np.float32)
    return jnp.asarray(sel3, BF16), jnp.asarray(tri3, BF16)


def _ssd(xbc, dt_raw, z, conv_w, conv_b, dt_bias, a_log, d_skip, norm_w):
    B, L, _ = xbc.shape
    tl = min(L, SSD_TILE)
    sel3, tri3 = _ssd_constants()

    def tok(width):
        return pl.BlockSpec((None, tl, width), lambda b, i: (b, i, 0))

    def pad_heads(p):
        return jnp.zeros((1, DT_PAD), F32).at[0, :SSD_HEADS].set(p)

    return pl.pallas_call(
        functools.partial(_ssd_kernel, tl=tl),
        grid=(B, L // tl),
        in_specs=[
            tok(CONV_CH), tok(DT_PAD), tok(SSD_WIDTH),
            _const_spec((SSD_CONV, CONV_CH)), _const_spec((1, CONV_CH)),
            _const_spec((1, DT_PAD)), _const_spec((1, DT_PAD)),
            _const_spec((1, SSD_WIDTH)), _const_spec((1, SSD_WIDTH)),
            _const_spec(sel3.shape), _const_spec(tri3.shape),
        ],
        out_specs=tok(SSD_WIDTH),
        out_shape=jax.ShapeDtypeStruct((B, L, SSD_WIDTH), BF16),
        scratch_shapes=[
            pltpu.VMEM((tl + SUBLANES, CONV_CH), F32),
            pltpu.VMEM((SSD_GROUPS, (SSD_HEADS // SSD_GROUPS) * SSD_HEAD_DIM, SSD_STATE), F32),
        ],
        compiler_params=_params("arbitrary", "arbitrary"),
        name="ssd_scan",
    )(xbc, dt_raw, z, conv_w, conv_b.reshape(1, CONV_CH), pad_heads(dt_bias), pad_heads(a_log),
      jnp.repeat(d_skip, SSD_HEAD_DIM).reshape(1, SSD_WIDTH), norm_w.reshape(1, SSD_WIDTH), sel3, tri3)


def _attn_kernel(lam_ref, q_ref, k_ref, v_ref, subw_ref, o_ref, s_ref, e_ref, vx_ref, *, seq, tq, lam_init):
    rows = 2 * tq
    vx_ref[:, 0:DIFF_V_DIM] = v_ref[...]
    vx_ref[:, DIFF_V_DIM:] = jnp.ones((seq, LANES), BF16)
    lp = lam_ref[...]
    lam = (jnp.exp(jnp.sum(lp[0:1] * lp[1:2], axis=-1, keepdims=True))
           - jnp.exp(jnp.sum(lp[2:3] * lp[3:4], axis=-1, keepdims=True)) + lam_init)
    lo_half = lax.broadcasted_iota(jnp.int32, (1, LANES), 1) < DIFF_HEAD_DIM
    qchunk = (lax.broadcasted_iota(jnp.int32, (rows, tq), 0) & (tq - 1)) >> 6
    kchunk = lax.broadcasted_iota(jnp.int32, (rows, tq), 1) >> 6
    allowed = kchunk <= qchunk
    for qi in reversed(range(seq // tq)):
        r0 = qi * tq
        ext = r0 + tq
        q = q_ref[r0:r0 + tq, :]
        zero = jnp.zeros_like(q)
        q2 = jnp.concatenate([jnp.where(lo_half, q, zero), jnp.where(lo_half, zero, q)], axis=0)
        blocks = [(c0, min(ATTN_KEY_BLOCK, r0 - c0)) for c0 in range(0, r0, ATTN_KEY_BLOCK)] + [(r0, tq)]
        for c0, cw in blocks:
            sc = _dot_nt(q2, k_ref[c0:c0 + cw, :])
            if c0 == r0:
                sc = jnp.where(allowed, sc, -jnp.inf)
            s_ref[:, c0:c0 + cw] = sc
        for rg in range(rows // ATTN_ROW_GROUP):
            rsl = slice(rg * ATTN_ROW_GROUP, (rg + 1) * ATTN_ROW_GROUP)
            mx = s_ref[rsl, 0:LANES]
            for j in range(1, ext // LANES):
                mx = jnp.maximum(mx, s_ref[rsl, j * LANES:(j + 1) * LANES])
            mb = jnp.broadcast_to(jnp.max(mx, axis=-1, keepdims=True), (ATTN_ROW_GROUP, LANES))
            for j in range(ext // LANES):
                csl = slice(j * LANES, (j + 1) * LANES)
                e_ref[rsl, csl] = jnp.exp2((s_ref[rsl, csl] - mb).astype(BF16))
        acc = _dot(e_ref[:, 0:ext], vx_ref[0:ext, :])
        inv = 1.0 / acc[:, DIFF_V_DIM:]
        acc = acc[:, :DIFF_V_DIM]
        o = acc[:tq] * inv[:tq] - acc[tq:] * (lam * inv[tq:])
        o = o * lax.rsqrt(jnp.mean(o * o, axis=-1, keepdims=True) + LN_EPS) * subw_ref[...] * (1.0 - lam_init)
        o_ref[r0:r0 + tq, :] = o.astype(BF16)


def _attention(q, k, v, diff_lambda, subln_w, lam_init):
    B, _, L, _ = q.shape
    tq = min(L, ATTN_Q_TILE)
    head = pl.BlockSpec((None, None, L, LANES), lambda b, h: (b, h, 0, 0))
    return pl.pallas_call(
        functools.partial(_attn_kernel, seq=L, tq=tq, lam_init=lam_init),
        grid=(B, DIFF_HEADS),
        in_specs=[_const_spec((4, DIFF_HEAD_DIM)), head, head, head, _const_spec((1, DIFF_V_DIM))],
        out_specs=head,
        out_shape=jax.ShapeDtypeStruct((B, DIFF_HEADS, L, LANES), BF16),
        scratch_shapes=[
            pltpu.VMEM((2 * tq, L), F32),
            pltpu.VMEM((2 * tq, L), BF16),
            pltpu.VMEM((L, DIFF_V_DIM + LANES), BF16),
        ],
        compiler_params=_params("arbitrary", "arbitrary"),
        name="diff_attention",
    )(diff_lambda, q, k, v, subln_w.reshape(1, DIFF_V_DIM))


def _pack_w_in(w):
    z, xbc, dt, q, k, v = jnp.split(w, (512, 1536, 1544, 2056, 2568), axis=-1)
    dt = jnp.pad(dt, ((0, 0),) * (w.ndim - 1) + ((0, DT_PAD - SSD_HEADS),))
    return jnp.concatenate([z, xbc, q, k, v, dt], axis=-1).astype(BF16)


def kernel(x, c, positions, w_ada, b_ada, w_ffn_in, w_ffn_out, w_in, conv_w, conv_b, dt_bias, a_log,
           d_skip, ssd_norm_w, diff_lambda, subln_w, w_out, ln_g, ln_b):
    B = x.shape[0]
    mod_all = _ada_mod(c, w_ada, b_ada).reshape(DEPTH, B, 3 * N_SUB, D_MODEL)
    cos, sin = _rope_tables(positions)
    w_ffn_in, w_ffn_out, w_out = (w.astype(BF16) for w in (w_ffn_in, w_ffn_out, w_out))
    w_proj = _pack_w_in(w_in)
    for layer in range(DEPTH):
        mod = mod_all[layer]
        lam_init = 0.8 - 0.6 * float(np.exp(-0.3 * layer))
        g = ln_g[layer].reshape(N_SUB, 1, D_MODEL)
        b = ln_b[layer].reshape(N_SUB, 1, D_MODEL)
        x = _ffn(x, mod, w_ffn_in, w_ffn_out, g[0], b[0], 0, layer)
        z, xbc, q, k, v, dt_raw = _inproj(x, mod, w_proj, cos, sin, layer)
        y = _ssd(xbc, dt_raw, z, conv_w[layer], conv_b[layer], dt_bias[layer], a_log[layer],
                 d_skip[layer], ssd_norm_w[layer])
        o = _attention(q, k, v, diff_lambda[layer], subln_w[layer], lam_init)
        x = _mixer_out_ffn(x, y, o, mod, w_out, g[1], b[1], w_ffn_in, w_ffn_out, g[2], b[2], layer)
    return x
```
